```python
import jax, jax.numpy as jnp
from jax import lax
import numpy as np

D_MODEL = 1024
BATCH = 4
SEQ = 8192
DEPTH = 1
DEC_BATCH = 32
DEC_SEQ = 64
PAST_LEN = 2048

CHUNK = 64
D_ATTN = D_MODEL // 2
SB_HEADS = 8
SB_HEAD_DIM = D_ATTN // SB_HEADS
SB_BLOCK = 128
D_POOL = D_MODEL - D_ATTN
POOL_WINDOWS = (2, 4, 8, 16)
N_POOL_GROUPS = len(POOL_WINDOWS)
POOL_GROUP = D_POOL // N_POOL_GROUPS
POOL_HIST = max(POOL_WINDOWS) - 1
D_IN = 3 * D_ATTN + D_POOL
N_KEYS = 128
N_EXPERTS = N_KEYS * N_KEYS
PEER_HEADS = 8
PEER_TOPK = 16
PEER_KEY_DIM = 256
KEY_HALF = PEER_KEY_DIM // 2
PEER_BLOCK = 256
EPS = 1e-6

kernel_name = 'hybrid_stickbreak_pool_peer_stream_step'


def _rms(x, g):
    xf = x.astype(jnp.float32)
    y = xf * lax.rsqrt(jnp.mean(xf * xf, axis=-1, keepdims=True) + EPS)
    return (y * g.astype(jnp.float32)).astype(x.dtype)


def _sb_block(q, qpos, k, v, kpos):
    z = jnp.einsum('bqhd,bkhd->bhqk', q, k).astype(jnp.float32) * (SB_HEAD_DIM ** -0.5)
    mask = kpos[None, :] < qpos[:, None]
    log_1mb = jnp.where(mask, jax.nn.log_sigmoid(-z), 0.0)
    after = lax.cumsum(log_1mb, axis=3, reverse=True) - log_1mb
    w = jnp.where(mask, jnp.exp(jax.nn.log_sigmoid(z) + after), 0.0)
    return jnp.einsum('bhqk,bkhd->bqhd', w.astype(v.dtype), v)


def _stick_breaking(q, k_all, v_all, offset):
    B, L, H, Dh = q.shape
    kpos = jnp.arange(k_all.shape[1])
    qpos = offset + jnp.arange(L)
    if L <= SB_BLOCK:
        return _sb_block(q, qpos, k_all, v_all, kpos)
    nb = L // SB_BLOCK
    qb = q.reshape(B, nb, SB_BLOCK, H, Dh).transpose(1, 0, 2, 3, 4)
    pb = qpos.reshape(nb, SB_BLOCK)
    ob = lax.map(lambda a: _sb_block(a[0], a[1], k_all, v_all, kpos), (qb, pb))
    return ob.transpose(1, 0, 2, 3, 4).reshape(B, L, H, Dh)


def _multiscale_pool(p_hist, p, offset, w_pool, pool_scale):
    B, L, _ = p.shape
    xp = jnp.concatenate([p_hist.astype(jnp.float32), p.astype(jnp.float32)], axis=1)
    cs = jnp.concatenate([jnp.zeros((B, 1, D_POOL), jnp.float32), jnp.cumsum(xp, axis=1)], axis=1)
    pos = offset + jnp.arange(L)
    end = cs[:, POOL_HIST + 1:POOL_HIST + 1 + L]
    outs = []
    for g, w in enumerate(POOL_WINDOWS):
        sl = slice(g * POOL_GROUP, (g + 1) * POOL_GROUP)
        start = cs[:, POOL_HIST + 1 - w:POOL_HIST + 1 - w + L, sl]
        cnt = jnp.minimum(pos + 1, w).astype(jnp.float32)[None, :, None]
        outs.append((end[..., sl] - start) / cnt)
    pooled = jnp.concatenate(outs, axis=-1) - p.astype(jnp.float32)
    mixed = jnp.einsum('blgc,gcd->blgd', pooled.reshape(B, L, N_POOL_GROUPS, POOL_GROUP),
                       w_pool.astype(jnp.float32)).reshape(B, L, D_POOL)
    return (mixed * pool_scale.astype(jnp.float32)).astype(p.dtype)


def _peer(h, w_query, sub_keys, u_tab, v_tab):
    B, L, D = h.shape
    T = B * L
    nb = -(-T // PEER_BLOCK)
    hp = jnp.pad(h.reshape(T, D), ((0, nb * PEER_BLOCK - T), (0, 0))).reshape(nb, PEER_BLOCK, D)

    def one(hb):
        q = (hb @ w_query).reshape(PEER_BLOCK, PEER_HEADS, 2, KEY_HALF)
        s1 = jnp.einsum('thd,hkd->thk', q[:, :, 0], sub_keys[0]).astype(jnp.float32)
        s2 = jnp.einsum('thd,hkd->thk', q[:, :, 1], sub_keys[1]).astype(jnp.float32)
        v1, i1 = lax.top_k(s1, PEER_TOPK)
        v2, i2 = lax.top_k(s2, PEER_TOPK)
        ncand = PEER_TOPK * PEER_TOPK
        cand = (v1[..., :, None] + v2[..., None, :]).reshape(PEER_BLOCK, PEER_HEADS, ncand)
        cidx = (i1[..., :, None] * N_KEYS + i2[..., None, :]).reshape(PEER_BLOCK, PEER_HEADS, ncand)
        best, sel = lax.top_k(cand, PEER_TOPK)
        idx = jnp.take_along_axis(cidx, sel, axis=-1)
        gate = jax.nn.softmax(best, axis=-1)
        ug = jnp.take(u_tab, idx, axis=0)
        vg = jnp.take(v_tab, idx, axis=0)
        act = jax.nn.gelu(jnp.einsum('td,thkd->thk', hb, ug).astype(jnp.float32), approximate=False)
        return jnp.einsum('thk,thkd->td', (gate * act).astype(hb.dtype), vg)

    return lax.map(one, hp).reshape(nb * PEER_BLOCK, D)[:T].reshape(B, L, D)


def _layer(x, c, k_hist, v_hist, p_hist, w_ada, b_ada, g_pre1, g_post1, g_pre2, g_post2,
           w_in, g_out_a, g_out_b, w_pool, pool_scale, w_out, w_query, sub_keys, u_experts, v_experts):
    B, L, _ = x.shape
    offset = k_hist.shape[1]
    mod = (jax.nn.silu(c) @ w_ada + b_ada).reshape(B, 6, D_MODEL)[:, :, None, :]
    sh1, sc1, gt1, sh2, sc2, gt2 = (mod[:, i] for i in range(6))
    h = _rms(x, g_pre1) * (1 + sc1) + sh1
    z = h @ w_in
    q, k, v, p = jnp.split(z, [D_ATTN, 2 * D_ATTN, 3 * D_ATTN], axis=-1)
    q = q.reshape(B, L, SB_HEADS, SB_HEAD_DIM)
    k = k.reshape(B, L, SB_HEADS, SB_HEAD_DIM)
    v = v.reshape(B, L, SB_HEADS, SB_HEAD_DIM)
    k_all = jnp.concatenate([k_hist.astype(k.dtype), k], axis=1)
    v_all = jnp.concatenate([v_hist.astype(v.dtype), v], axis=1)
    o_a = _stick_breaking(q, k_all, v_all, offset).reshape(B, L, D_ATTN)
    o_b = _multiscale_pool(p_hist, p, offset, w_pool, pool_scale)
    o = jnp.concatenate([_rms(o_a, g_out_a), _rms(o_b, g_out_b)], axis=-1) @ w_out
    x = x + gt1 * _rms(o, g_post1)
    h2 = _rms(x, g_pre2) * (1 + sc2) + sh2
    x = x + gt2 * _rms(_peer(h2, w_query, sub_keys, u_experts, v_experts), g_post2)
    p_state = jnp.concatenate([p_hist.astype(p.dtype), p], axis=1)[:, -POOL_HIST:]
    return x, k, v, p_state


def setup_inputs(seed: int = 0) -> dict:
    key = jax.random.key(seed)
    ks = jax.random.split(key, 24)
    n = lambda i, shape: jax.random.normal(ks[i], shape, jnp.float32)
    return {
        'x_prompt': n(0, (BATCH, SEQ, D_MODEL)),
        'x_sample': n(1, (DEC_BATCH, DEC_SEQ, D_MODEL)),
        'cache_k': n(2, (DEPTH, DEC_BATCH, PAST_LEN, SB_HEADS, SB_HEAD_DIM)),
        'cache_v': n(3, (DEPTH, DEC_BATCH, PAST_LEN, SB_HEADS, SB_HEAD_DIM)),
        'state_pool': n(4, (DEPTH, DEC_BATCH, POOL_HIST, D_POOL)),
        'c_prompt': n(5, (BATCH, D_MODEL)),
        'c_sample': n(6, (DEC_BATCH, D_MODEL)),
        'w_ada': n(7, (DEPTH, D_MODEL, 6 * D_MODEL)) * (0.5 * D_MODEL ** -0.5),
        'b_ada': n(8, (DEPTH, 6 * D_MODEL)) * 0.01,
        'g_pre1': 1.0 + 0.05 * n(9, (DEPTH, D_MODEL)),
        'g_post1': 1.0 + 0.05 * n(10, (DEPTH, D_MODEL)),
        'g_pre2': 1.0 + 0.05 * n(11, (DEPTH, D_MODEL)),
        'g_post2': 1.0 + 0.05 * n(12, (DEPTH, D_MODEL)),
        'w_in': n(13, (DEPTH, D_MODEL, D_IN)) * D_MODEL ** -0.5,
        'g_out_a': 1.0 + 0.05 * n(14, (DEPTH, D_ATTN)),
        'g_out_b': 1.0 + 0.05 * n(15, (DEPTH, D_POOL)),
        'w_pool': n(16, (DEPTH, N_POOL_GROUPS, POOL_GROUP, POOL_GROUP)) * POOL_GROUP ** -0.5,
        'pool_scale': 1.0 + 0.1 * n(17, (DEPTH, D_POOL)),
        'w_out': n(18, (DEPTH, D_MODEL, D_MODEL)) * D_MODEL ** -0.5,
        'w_query': n(19, (DEPTH, D_MODEL, PEER_HEADS * PEER_KEY_DIM)) * D_MODEL ** -0.5,
        'sub_keys': n(20, (DEPTH, 2, PEER_HEADS, N_KEYS, KEY_HALF)) * KEY_HALF ** -0.5,
        'u_experts': n(21, (DEPTH, N_EXPERTS, D_MODEL)) * D_MODEL ** -0.5,
        'v_experts': n(22, (DEPTH, N_EXPERTS, D_MODEL)) * D_MODEL ** -0.5,
    }


def reference(x_prompt, x_sample, cache_k, cache_v, state_pool, c_prompt, c_sample, w_ada, b_ada,
              g_pre1, g_post1, g_pre2, g_post2, w_in, g_out_a, g_out_b, w_pool, pool_scale, w_out,
              w_query, sub_keys, u_experts, v_experts):
    y_p, y_s = x_prompt, x_sample
    bp = x_prompt.shape[0]
    kp_l, vp_l, pp_l, ks_l, vs_l, ps_l = [], [], [], [], [], []
    for l in range(DEPTH):
        wts = (w_ada[l], b_ada[l], g_pre1[l], g_post1[l], g_pre2[l], g_post2[l], w_in[l],
               g_out_a[l], g_out_b[l], w_pool[l], pool_scale[l], w_out[l], w_query[l],
               sub_keys[l], u_experts[l], v_experts[l])
        empty_kv = jnp.zeros((bp, 0, SB_HEADS, SB_HEAD_DIM), y_p.dtype)
        empty_pool = jnp.zeros((bp, POOL_HIST, D_POOL), y_p.dtype)
        y_p, kp, vp, pp = _layer(y_p, c_prompt, empty_kv, empty_kv, empty_pool, *wts)
        y_s, ks_new, vs_new, ps = _layer(y_s, c_sample, cache_k[l], cache_v[l], state_pool[l], *wts)
        kp_l.append(kp); vp_l.append(vp); pp_l.append(pp)
        ks_l.append(ks_new); vs_l.append(vs_new); ps_l.append(ps)
    return (y_p, y_s, jnp.stack(kp_l), jnp.stack(vp_l), jnp.stack(pp_l),
            jnp.stack(ks_l), jnp.stack(vs_l), jnp.stack(ps_l))
```

```python
import functools

import jax
import jax.numpy as jnp
from jax import lax
from jax.experimental import pallas as pl
from jax.experimental.pallas import tpu as pltpu

F32 = jnp.float32
BF16 = jnp.bfloat16

D_MODEL = 1024
D_ATTN = D_MODEL // 2
SB_HEADS = 8
SB_HEAD_DIM = D_ATTN // SB_HEADS
D_POOL = D_MODEL - D_ATTN
POOL_WINDOWS = (2, 4, 8, 16)
POOL_GROUP = D_POOL // len(POOL_WINDOWS)
POOL_HIST = max(POOL_WINDOWS) - 1
POOL_TAIL = POOL_HIST + 1
D_IN = 3 * D_ATTN + D_POOL
N_KEYS = 128
PEER_HEADS = 8
PEER_TOPK = 16
KEY_HALF = 128
PEER_SEL = PEER_HEADS * PEER_TOPK
EPS = 1e-6

LANES = 128
HEADS_PER_LANE_TILE = LANES // SB_HEAD_DIM
SB_KEY_BLOCK = 128
EXP_UNDERFLOW = -104.0

TOKEN_BLOCK = 256
PEER_TOKEN_BLOCK = 128
PEER_SLOTS = 8
VMEM_LIMIT = 48 * 1024 * 1024


def _params(*sem):
    return pltpu.CompilerParams(dimension_semantics=sem, vmem_limit_bytes=VMEM_LIMIT)


def _rms(x, g):
    ms = jnp.mean(x * x, axis=-1, keepdims=True)
    return x * lax.rsqrt(ms + EPS) * g


def _gelu(x):
    return 0.5 * x * (1.0 + lax.erf(x * (2.0 ** -0.5)))


def _mod_kernel(c_ref, w_ref, b_ref, o_ref):
    s = jax.nn.silu(c_ref[...])
    o_ref[...] = jnp.dot(s, w_ref[...], precision=lax.Precision.HIGHEST,
                         preferred_element_type=F32) + b_ref[...]


def _mod(c, w_ada, b_ada):
    n, d = c.shape
    dout = w_ada.shape[1]
    return pl.pallas_call(
        _mod_kernel,
        grid=(dout // d,),
        in_specs=[pl.BlockSpec((n, d), lambda j: (0, 0)),
                  pl.BlockSpec((d, d), lambda j: (0, j)),
                  pl.BlockSpec((1, d), lambda j: (0, j))],
        out_specs=pl.BlockSpec((n, d), lambda j: (0, j)),
        out_shape=jax.ShapeDtypeStruct((n, dout), F32),
        compiler_params=_params("arbitrary"),
        name="mod",
    )(c, w_ada, b_ada.reshape(1, dout))


def _inproj_kernel(x_ref, mod_ref, g_ref, w_ref, q_ref, k_ref, v_ref, p_ref, kb_ref, vb_ref):
    nb, tb, d = x_ref.shape
    h = _rms(x_ref[...], g_ref[...]) * (1.0 + mod_ref[:, 1:2, :]) + mod_ref[:, 0:1, :]
    z = jnp.dot(h.astype(BF16).reshape(nb * tb, d), w_ref[...], preferred_element_type=F32)
    z = z.reshape(nb, tb, D_IN)
    k = z[:, :, D_ATTN:2 * D_ATTN]
    v = z[:, :, 2 * D_ATTN:3 * D_ATTN]
    q_ref[...] = (z[:, :, :D_ATTN] * (SB_HEAD_DIM ** -0.5)).astype(BF16)
    k_ref[...] = k
    v_ref[...] = v
    p_ref[...] = z[:, :, 3 * D_ATTN:]
    kb_ref[...] = k.astype(BF16)
    vb_ref[...] = v.astype(BF16)


def _token_blocking(n_batch, length):
    tb = min(length, TOKEN_BLOCK)
    nb = TOKEN_BLOCK // tb
    assert length % tb == 0 and n_batch % nb == 0 and tb % 8 == 0
    return nb, tb


def _inproj(x, mod, g_pre1, w_in_bf):
    n, length, d = x.shape
    nb, tb = _token_blocking(n, length)
    tok = lambda w: pl.BlockSpec((nb, tb, w), lambda b, i: (b, i, 0))
    out = lambda dt: jax.ShapeDtypeStruct((n, length, D_ATTN), dt)
    return pl.pallas_call(
        _inproj_kernel,
        grid=(n // nb, length // tb),
        in_specs=[tok(d),
                  pl.BlockSpec((nb, 6, d), lambda b, i: (b, 0, 0)),
                  pl.BlockSpec((1, d), lambda b, i: (0, 0)),
                  pl.BlockSpec((d, D_IN), lambda b, i: (0, 0))],
        out_specs=[tok(D_ATTN)] * 6,
        out_shape=[out(BF16), out(F32), out(F32), out(F32), out(BF16), out(BF16)],
        compiler_params=_params("arbitrary", "arbitrary"),
        name="inproj",
    )(x, mod, g_pre1, w_in_bf)


def _suffix_matrix(tk):
    shape = (2 * tk, tk + LANES)
    r = lax.broadcasted_iota(jnp.int32, shape, 0)
    r = jnp.where(r >= tk, r - tk, r)
    c = lax.broadcasted_iota(jnp.int32, shape, 1)
    return jnp.where((r > c) | (c >= tk), 1.0, 0.0).astype(BF16)


def _sb_step(z, vb, carry, acc, suffix_mat, mask):
    tk = z.shape[1]
    sp = jnp.maximum(z, 0.0) + jnp.log1p(jnp.exp(-jnp.abs(z)))
    l = -sp if mask is None else jnp.where(mask, -sp, 0.0)
    l_hi = l.astype(BF16)
    l_lo = (l - l_hi.astype(F32)).astype(BF16)
    st = jnp.dot(jnp.concatenate([l_hi, l_lo], axis=1), suffix_mat, preferred_element_type=F32)
    e = z - sp + st[:, :tk]
    if carry is not None:
        e = e + carry[:, :tk] if tk < LANES else e + carry
    w = jnp.exp(e)
    if mask is not None:
        w = jnp.where(mask, w, 0.0)
    pv = jnp.dot(w.astype(BF16), vb, preferred_element_type=F32)
    total = st[:, tk:]
    return (total if carry is None else carry + total), (pv if acc is None else acc + pv)


def _attn_kernel(*refs, n_hist_blocks):
    if n_hist_blocks:
        q_ref, k_ref, v_ref, hk_ref, hv_ref, o_ref = refs
    else:
        q_ref, k_ref, v_ref, o_ref = refs
    tq = q_ref.shape[1]
    i = pl.program_id(2)
    new_mat = _suffix_matrix(tq)
    blk_mat = new_mat if tq == SB_KEY_BLOCK else _suffix_matrix(SB_KEY_BLOCK)
    causal = (lax.broadcasted_iota(jnp.int32, (tq, tq), 1) < lax.broadcasted_iota(jnp.int32, (tq, tq), 0))
    lane = lax.broadcasted_iota(jnp.int32, (tq, LANES), 1)
    q = q_ref[0]
    nt = (((1,), (1,)), ((), ()))

    def sweep(state, read_block):
        def body(s):
            j, carry, acc, _ = s
            kb, vb = read_block(j)
            z = lax.dot_general(qh, kb, nt, preferred_element_type=F32)
            carry, acc = _sb_step(z, vb, carry, acc, blk_mat, None)
            return j - 1, carry, acc, jnp.max(carry)
        return lax.while_loop(lambda s: (s[0] >= 0) & (s[3] > EXP_UNDERFLOW), body, state)

    heads = []
    for h in range(HEADS_PER_LANE_TILE):
        in_head = (lane >= h * SB_HEAD_DIM) & (lane < (h + 1) * SB_HEAD_DIM)
        qh = jnp.where(in_head, q, jnp.zeros_like(q))
        start = pl.multiple_of(i * tq, tq)
        z = lax.dot_general(qh, k_ref[0, pl.ds(start, tq), :], nt, preferred_element_type=F32)
        carry, acc = _sb_step(z, v_ref[0, pl.ds(start, tq), :], None, None, new_mat, causal)
        state = (i - 1, carry, acc, jnp.max(carry))

        def read_new(j):
            s = pl.multiple_of(j * SB_KEY_BLOCK, SB_KEY_BLOCK)
            return k_ref[0, pl.ds(s, SB_KEY_BLOCK), :], v_ref[0, pl.ds(s, SB_KEY_BLOCK), :]

        def read_hist(j):
            s = pl.multiple_of(j * SB_KEY_BLOCK, SB_KEY_BLOCK)
            return (hk_ref[0, pl.ds(s, SB_KEY_BLOCK), :].astype(BF16),
                    hv_ref[0, pl.ds(s, SB_KEY_BLOCK), :].astype(BF16))

        if tq == SB_KEY_BLOCK:
            state = sweep(state, read_new)
        else:
            assert q_ref.shape[1] == k_ref.shape[1]
        if n_hist_blocks:
            state = sweep((jnp.int32(n_hist_blocks - 1),) + state[1:], read_hist)
        heads.append((in_head, state[2]))
    out = heads[-1][1]
    for in_head, acc in heads[:-1]:
        out = jnp.where(in_head, acc, out)
    o_ref[0] = out


def _attn(q_bf, k_bf, v_bf, hist_k=None, hist_v=None):
    n, length, _ = q_bf.shape
    tq = min(length, SB_KEY_BLOCK)
    n_tiles = D_ATTN // LANES
    qspec = pl.BlockSpec((1, tq, LANES), lambda b, t, i: (b, i, t))
    kspec = pl.BlockSpec((1, length, LANES), lambda b, t, i: (b, 0, t))
    in_specs = [qspec, kspec, kspec]
    args = [q_bf, k_bf, v_bf]
    n_hist_blocks = 0
    if hist_k is not None:
        past = hist_k.shape[1]
        assert past % SB_KEY_BLOCK == 0
        n_hist_blocks = past // SB_KEY_BLOCK
        hspec = pl.BlockSpec((1, past, LANES), lambda b, t, i: (b, 0, t))
        in_specs += [hspec, hspec]
        args += [hist_k, hist_v]
    return pl.pallas_call(
        functools.partial(_attn_kernel, n_hist_blocks=n_hist_blocks),
        grid=(n, n_tiles, length // tq),
        in_specs=in_specs,
        out_specs=qspec,
        out_shape=jax.ShapeDtypeStruct((n, length, D_ATTN), F32),
        compiler_params=_params("arbitrary", "arbitrary", "arbitrary"),
        name="attn",
    )(*args)


def _post_kernel(oa_ref, p_ref, pprev_ref, hist_ref, x_ref, mod_ref, goa_ref, gob_ref, wpool_ref,
                 pscale_ref, wout_ref, gpost1_ref, gpre2_ref, wq_ref,
                 x1_ref, h2_ref, qp_ref, xp_ref, *, offset):
    nb, tb, d = x_ref.shape
    i = pl.program_id(1)
    p = p_ref[...]
    first = jnp.broadcast_to(i == 0, (nb, POOL_TAIL, D_POOL))
    xp_ref[:, :POOL_TAIL, :] = jnp.where(first, hist_ref[...], pprev_ref[:, tb - POOL_TAIL:, :])
    xp_ref[:, POOL_TAIL:, :] = p
    pos = offset + i * tb + lax.broadcasted_iota(jnp.int32, (1, tb, 1), 1)
    mixed = []
    for g, w in enumerate(POOL_WINDOWS):
        lanes = slice(g * POOL_GROUP, (g + 1) * POOL_GROUP)
        s = p[:, :, lanes]
        for back in range(1, w):
            s = s + xp_ref[:, pl.ds(POOL_TAIL - back, tb), lanes]
        cnt = jnp.minimum(pos + 1, w).astype(F32)
        pooled = (s / cnt - p[:, :, lanes]).reshape(nb * tb, POOL_GROUP)
        mixed.append(jnp.dot(pooled.astype(BF16), wpool_ref[g], preferred_element_type=F32))
    o_b = jnp.concatenate(mixed, axis=-1) * pscale_ref[...]
    o_a = oa_ref[...].reshape(nb * tb, D_ATTN)
    cat = jnp.concatenate([_rms(o_a, goa_ref[...]), _rms(o_b, gob_ref[...])], axis=-1)
    o = jnp.dot(cat.astype(BF16), wout_ref[...], preferred_element_type=F32)
    x1 = x_ref[...] + mod_ref[:, 2:3, :] * _rms(o, gpost1_ref[...]).reshape(nb, tb, d)
    x1_ref[...] = x1
    h2 = _rms(x1, gpre2_ref[...]) * (1.0 + mod_ref[:, 4:5, :]) + mod_ref[:, 3:4, :]
    h2_ref[...] = h2
    qp = jnp.dot(h2.astype(BF16).reshape(nb * tb, d), wq_ref[...], preferred_element_type=F32)
    qp_ref[...] = qp.reshape(nb, tb, qp_ref.shape[2])


def _post(o_a, p, hist, x, mod, g_out_a, g_out_b, w_pool_bf, pool_scale, w_out_bf, g_post1, g_pre2,
          w_query_bf, offset):
    n, length, d = x.shape
    nb, tb = _token_blocking(n, length)
    assert tb >= POOL_TAIL
    dq = w_query_bf.shape[1]
    tok = lambda w: pl.BlockSpec((nb, tb, w), lambda b, i: (b, i, 0))
    full = lambda a: pl.BlockSpec(a.shape, lambda b, i: (0,) * a.ndim)
    return pl.pallas_call(
        functools.partial(_post_kernel, offset=offset),
        grid=(n // nb, length // tb),
        in_specs=[tok(D_ATTN), tok(D_POOL),
                  pl.BlockSpec((nb, tb, D_POOL), lambda b, i: (b, jnp.maximum(i - 1, 0), 0)),
                  pl.BlockSpec((nb, POOL_TAIL, D_POOL), lambda b, i: (b, 0, 0)),
                  tok(d),
                  pl.BlockSpec((nb, 6, d), lambda b, i: (b, 0, 0)),
                  full(g_out_a), full(g_out_b), full(w_pool_bf), full(pool_scale), full(w_out_bf),
                  full(g_post1), full(g_pre2), full(w_query_bf)],
        out_specs=[tok(d), tok(d), tok(dq)],
        out_shape=[jax.ShapeDtypeStruct((n, length, d), F32),
                   jax.ShapeDtypeStruct((n, length, d), F32),
                   jax.ShapeDtypeStruct((n, length, dq), F32)],
        scratch_shapes=[pltpu.VMEM((nb, tb + POOL_TAIL, D_POOL), F32)],
        compiler_params=_params("arbitrary", "arbitrary"),
        name="post",
    )(o_a, p, p, hist, x, mod, g_out_a, g_out_b, w_pool_bf, pool_scale, w_out_bf, g_post1, g_pre2,
      w_query_bf)


def _topk_rows(s, k):
    rows = s.shape[0]
    row = lax.broadcasted_iota(jnp.int32, s.shape, 0)
    vals, idxs = [], []
    for _ in range(k):
        m = jnp.max(s, axis=0, keepdims=True)
        am = jnp.min(jnp.where(s == m, row, rows), axis=0, keepdims=True)
        vals.append(m)
        idxs.append(am)
        s = jnp.where(row == am, -jnp.inf, s)
    return jnp.concatenate(vals, axis=0), jnp.concatenate(idxs, axis=0)


def _select_rows(table, sel):
    out = jnp.zeros_like(table)
    for r in range(table.shape[0]):
        out = jnp.where(sel == r, table[r:r + 1, :], out)
    return out


def _route_kernel(qp_ref, keys_ref, idx_ref, gate_ref):
    nt = (((1,), (1,)), ((), ()))
    idx_rows, gate_rows = [], []
    for h in range(PEER_HEADS):
        base = h * 2 * KEY_HALF
        q1 = qp_ref[:, base:base + KEY_HALF].astype(BF16)
        q2 = qp_ref[:, base + KEY_HALF:base + 2 * KEY_HALF].astype(BF16)
        s1 = lax.dot_general(keys_ref[0, h], q1, nt, preferred_element_type=F32)
        s2 = lax.dot_general(keys_ref[1, h], q2, nt, preferred_element_type=F32)
        v1, i1 = _topk_rows(s1, PEER_TOPK)
        v2, i2 = _topk_rows(s2, PEER_TOPK)
        cand = jnp.concatenate([v1[a:a + 1, :] + v2 for a in range(PEER_TOPK)], axis=0)
        best, sel = _topk_rows(cand, PEER_TOPK)
        a_sel = lax.shift_right_logical(sel, 4)
        b_sel = sel & (PEER_TOPK - 1)
        idx_rows.append(_select_rows(i1, a_sel) * N_KEYS + _select_rows(i2, b_sel))
        e = jnp.exp(best - best[0:1, :])
        gate_rows.append(e / jnp.sum(e, axis=0, keepdims=True))
    idx_ref[...] = jnp.concatenate(idx_rows, axis=0).T
    gate_ref[...] = jnp.concatenate(gate_rows, axis=0).T


def _route(qp, sub_keys_bf):
    t, dq = qp.shape
    tb = min(t, TOKEN_BLOCK)
    assert t % tb == 0 and PEER_TOPK == 16
    return pl.pallas_call(
        _route_kernel,
        grid=(t // tb,),
        in_specs=[pl.BlockSpec((tb, dq), lambda i: (i, 0)),
                  pl.BlockSpec(sub_keys_bf.shape, lambda i: (0, 0, 0, 0))],
        out_specs=[pl.BlockSpec((tb, PEER_SEL), lambda i: (i, 0))] * 2,
        out_shape=[jax.ShapeDtypeStruct((t, PEER_SEL), jnp.int32),
                   jax.ShapeDtypeStruct((t, PEER_SEL), F32)],
        compiler_params=_params("arbitrary"),
        name="route",
    )(qp, sub_keys_bf)


def _peer_kernel(idx_ref, gate_ref, h_ref, uv_ref, o_ref, buf_ref, sem_ref):
    tg, d = h_ref.shape
    eye = (lax.broadcasted_iota(jnp.int32, (PEER_SEL, PEER_SEL), 0)
           == lax.broadcasted_iota(jnp.int32, (PEER_SEL, PEER_SEL), 1))

    def issue(t, slot):
        for j in range(PEER_SEL):
            pltpu.make_async_copy(uv_ref.at[pl.ds(idx_ref[t, j], 1)],
                                  buf_ref.at[slot, pl.ds(j, 1)], sem_ref.at[slot]).start()

    def wait(slot):
        pltpu.make_async_copy(uv_ref.at[pl.ds(0, PEER_SEL)], buf_ref.at[slot], sem_ref.at[slot]).wait()

    def prologue(s, carry):
        issue(s, s)
        return carry

    lax.fori_loop(0, PEER_SLOTS, prologue, 0)

    def body(t, carry):
        slot = t & (PEER_SLOTS - 1)
        wait(slot)
        hrow = h_ref[pl.ds(t, 1), :]
        s_col = jnp.sum(buf_ref[slot, :, :d] * hrow, axis=1, keepdims=True)
        s_row = jnp.sum(jnp.where(eye, s_col, 0.0), axis=0, keepdims=True)
        a_row = gate_ref[pl.ds(t, 1), :] * _gelu(s_row)
        a_col = jnp.sum(jnp.where(eye, a_row, 0.0), axis=1, keepdims=True)
        o_ref[pl.ds(t, 1), :] = jnp.sum(a_col * buf_ref[slot, :, d:], axis=0, keepdims=True)

        @pl.when(t + PEER_SLOTS < tg)
        def _():
            issue(t + PEER_SLOTS, slot)
        return carry

    lax.fori_loop(0, tg, body, 0)


def _peer(idx, gate, h2, uv):
    t, d = h2.shape
    tg = PEER_TOKEN_BLOCK
    assert t % tg == 0 and tg >= PEER_SLOTS and PEER_SLOTS & (PEER_SLOTS - 1) == 0
    return pl.pallas_call(
        _peer_kernel,
        grid=(t // tg,),
        in_specs=[pl.BlockSpec((tg, PEER_SEL), lambda i: (i, 0), memory_space=pltpu.SMEM),
                  pl.BlockSpec((tg, PEER_SEL), lambda i: (i, 0)),
                  pl.BlockSpec((tg, d), lambda i: (i, 0)),
                  pl.BlockSpec(memory_space=pl.ANY)],
        out_specs=pl.BlockSpec((tg, d), lambda i: (i, 0)),
        out_shape=jax.ShapeDtypeStruct((t, d), F32),
        scratch_shapes=[pltpu.VMEM((PEER_SLOTS, PEER_SEL, 2 * d), F32),
                        pltpu.SemaphoreType.DMA((PEER_SLOTS,))],
        compiler_params=_params("arbitrary"),
        name="peer",
    )(idx, gate, h2, uv)


def _final_kernel(x1_ref, peer_ref, mod_ref, g_ref, y_ref):
    y_ref[...] = x1_ref[...] + mod_ref[:, 5:6, :] * _rms(peer_ref[...], g_ref[...])


def _final(x1, peer, mod, g_post2):
    n, length, d = x1.shape
    nb, tb = _token_blocking(n, length)
    tok = pl.BlockSpec((nb, tb, d), lambda b, i: (b, i, 0))
    return pl.pallas_call(
        _final_kernel,
        grid=(n // nb, length // tb),
        in_specs=[tok, tok, pl.BlockSpec((nb, 6, d), lambda b, i: (b, 0, 0)),
                  pl.BlockSpec((1, d), lambda b, i: (0, 0))],
        out_specs=tok,
        out_shape=jax.ShapeDtypeStruct((n, length, d), F32),
        compiler_params=_params("arbitrary", "arbitrary"),
        name="final",
    )(x1, peer, mod, g_post2)


def _layer(x, mod, hist_k, hist_v, hist_p, wts):
    (g_pre1, g_post1, g_pre2, g_post2, w_in_bf, g_out_a, g_out_b, w_pool_bf, pool_scale, w_out_bf,
     w_query_bf, sub_keys_bf, uv) = wts
    n, length, d = x.shape
    q_bf, k, v, p, k_bf, v_bf = _inproj(x, mod, g_pre1, w_in_bf)
    offset = 0 if hist_k is None else hist_k.shape[1]
    o_a = _attn(q_bf, k_bf, v_bf, hist_k, hist_v)
    x1, h2, qp = _post(o_a, p, hist_p, x, mod, g_out_a, g_out_b, w_pool_bf, pool_scale, w_out_bf,
                       g_post1, g_pre2, w_query_bf, offset)
    idx, gate = _route(qp.reshape(n * length, -1), sub_keys_bf)
    peer = _peer(idx, gate, h2.reshape(n * length, d), uv)
    y = _final(x1, peer.reshape(n, length, d), mod, g_post2)
    heads = (n, length, SB_HEADS, SB_HEAD_DIM)
    p_state = jnp.concatenate([hist_p[:, 1:], p], axis=1)[:, -POOL_HIST:]
    return y, k.reshape(heads), v.reshape(heads), p_state


def kernel(x_prompt, x_sample, cache_k, cache_v, state_pool, c_prompt, c_sample, w_ada, b_ada, g_pre1, g_post1, g_pre2, g_post2, w_in, g_out_a, g_out_b, w_pool, pool_scale, w_out, w_query, sub_keys, u_experts, v_experts):
    depth = w_ada.shape[0]
    assert depth == 1
    bp, bs = x_prompt.shape[0], x_sample.shape[0]
    row = lambda a: a.reshape(1, -1)
    c_all = jnp.concatenate([c_prompt, c_sample], axis=0)
    pad = (-c_all.shape[0]) % 8
    mod = _mod(jnp.pad(c_all, ((0, pad), (0, 0))), w_ada[0], b_ada[0]).reshape(-1, 6, D_MODEL)
    wts = (row(g_pre1[0]), row(g_post1[0]), row(g_pre2[0]), row(g_post2[0]), w_in[0].astype(BF16),
           row(g_out_a[0]), row(g_out_b[0]), w_pool[0].astype(BF16), row(pool_scale[0]),
           w_out[0].astype(BF16), w_query[0].astype(BF16), sub_keys[0].astype(BF16),
           jnp.concatenate([u_experts[0], v_experts[0]], axis=1))
    past = cache_k.shape[2]
    hist_p = jnp.pad(state_pool[0], ((0, 0), (POOL_TAIL - POOL_HIST, 0), (0, 0)))
    y_p, k_p, v_p, p_p = _layer(x_prompt, mod[:bp], None, None,
                                jnp.zeros((bp, POOL_TAIL, D_POOL), F32), wts)
    y_s, k_s, v_s, p_s = _layer(x_sample, mod[bp:bp + bs], cache_k[0].reshape(bs, past, D_ATTN),
                                cache_v[0].reshape(bs, past, D_ATTN), hist_p, wts)
    return (y_p, y_s, k_p[None], v_p[None], p_p[None], k_s[None], v_s[None], p_s[None])
```

```python
import functools

import jax
import jax.numpy as jnp
from jax import lax
from jax.experimental import pallas as pl
from jax.experimental.pallas import tpu as pltpu

F32 = jnp.float32
BF16 = jnp.bfloat16

D_MODEL = 1024
D_ATTN = D_MODEL // 2
SB_HEADS = 8
SB_HEAD_DIM = D_ATTN // SB_HEADS
D_POOL = D_MODEL - D_ATTN
POOL_WINDOWS = (2, 4, 8, 16)
POOL_GROUP = D_POOL // len(POOL_WINDOWS)
POOL_HIST = max(POOL_WINDOWS) - 1
POOL_TAIL = POOL_HIST + 1
D_IN = 3 * D_ATTN + D_POOL
N_KEYS = 128
PEER_HEADS = 8
PEER_TOPK = 16
KEY_HALF = 128
PEER_SEL = PEER_HEADS * PEER_TOPK
EPS = 1e-6

LANES = 128
HEADS_PER_LANE_TILE = LANES // SB_HEAD_DIM
SB_KEY_BLOCK = 128
EXP_UNDERFLOW = -104.0

TOKEN_BLOCK = 256
PEER_TOKEN_BLOCK = 128
PEER_SLOTS = 8
VMEM_LIMIT = 48 * 1024 * 1024


def _params(*sem):
    return pltpu.CompilerParams(dimension_semantics=sem, vmem_limit_bytes=VMEM_LIMIT)


def _rms(x, g):
    ms = jnp.mean(x * x, axis=-1, keepdims=True)
    return x * lax.rsqrt(ms + EPS) * g


def _gelu(x):
    return 0.5 * x * (1.0 + lax.erf(x * (2.0 ** -0.5)))


def _mod_kernel(c_ref, w_ref, b_ref, o_ref):
    s = jax.nn.silu(c_ref[...])
    o_ref[...] = jnp.dot(s, w_ref[...], precision=lax.Precision.HIGHEST,
                         preferred_element_type=F32) + b_ref[...]


def _mod(c, w_ada, b_ada):
    n, d = c.shape
    dout = w_ada.shape[1]
    return pl.pallas_call(
        _mod_kernel,
        grid=(dout // d,),
        in_specs=[pl.BlockSpec((n, d), lambda j: (0, 0)),
                  pl.BlockSpec((d, d), lambda j: (0, j)),
                  pl.BlockSpec((1, d), lambda j: (0, j))],
        out_specs=pl.BlockSpec((n, d), lambda j: (0, j)),
        out_shape=jax.ShapeDtypeStruct((n, dout), F32),
        compiler_params=_params("arbitrary"),
        name="mod",
    )(c, w_ada, b_ada.reshape(1, dout))


def _inproj_kernel(x_ref, mod_ref, g_ref, w_ref, q_ref, k_ref, v_ref, p_ref, kb_ref, vb_ref):
    nb, tb, d = x_ref.shape
    h = _rms(x_ref[...], g_ref[...]) * (1.0 + mod_ref[:, 1:2, :]) + mod_ref[:, 0:1, :]
    z = jnp.dot(h.astype(BF16).reshape(nb * tb, d), w_ref[...], preferred_element_type=F32)
    z = z.reshape(nb, tb, D_IN)
    k = z[:, :, D_ATTN:2 * D_ATTN]
    v = z[:, :, 2 * D_ATTN:3 * D_ATTN]
    q_ref[...] = (z[:, :, :D_ATTN] * (SB_HEAD_DIM ** -0.5)).astype(BF16)
    k_ref[...] = k
    v_ref[...] = v
    p_ref[...] = z[:, :, 3 * D_ATTN:]
    kb_ref[...] = k.astype(BF16)
    vb_ref[...] = v.astype(BF16)


def _token_blocking(n_batch, length):
    tb = min(length, TOKEN_BLOCK)
    nb = TOKEN_BLOCK // tb
    assert length % tb == 0 and n_batch % nb == 0 and tb % 8 == 0
    return nb, tb


def _inproj(x, mod, g_pre1, w_in_bf):
    n, length, d = x.shape
    nb, tb = _token_blocking(n, length)
    tok = lambda w: pl.BlockSpec((nb, tb, w), lambda b, i: (b, i, 0))
    out = lambda dt: jax.ShapeDtypeStruct((n, length, D_ATTN), dt)
    return pl.pallas_call(
        _inproj_kernel,
        grid=(n // nb, length // tb),
        in_specs=[tok(d),
                  pl.BlockSpec((nb, 6, d), lambda b, i: (b, 0, 0)),
                  pl.BlockSpec((1, d), lambda b, i: (0, 0)),
                  pl.BlockSpec((d, D_IN), lambda b, i: (0, 0))],
        out_specs=[tok(D_ATTN)] * 6,
        out_shape=[out(BF16), out(F32), out(F32), out(F32), out(BF16), out(BF16)],
        compiler_params=_params("arbitrary", "arbitrary"),
        name="inproj",
    )(x, mod, g_pre1, w_in_bf)


def _suffix_matrix(tk):
    shape = (2 * tk, tk + LANES)
    r = lax.broadcasted_iota(jnp.int32, shape, 0)
    r = jnp.where(r >= tk, r - tk, r)
    c = lax.broadcasted_iota(jnp.int32, shape, 1)
    return jnp.where((r > c) | (c >= tk), 1.0, 0.0).astype(BF16)


def _sb_step(z, vb, carry, acc, suffix_mat, mask):
    tk = z.shape[1]
    sp = jnp.maximum(z, 0.0) + jnp.log1p(jnp.exp(-jnp.abs(z)))
    l = -sp if mask is None else jnp.where(mask, -sp, 0.0)
    l_hi = l.astype(BF16)
    l_lo = (l - l_hi.astype(F32)).astype(BF16)
    st = jnp.dot(jnp.concatenate([l_hi, l_lo], axis=1), suffix_mat, preferred_element_type=F32)
    e = z - sp + st[:, :tk]
    if carry is not None:
        e = e + carry[:, :tk] if tk < LANES else e + carry
    w = jnp.exp(e)
    if mask is not None:
        w = jnp.where(mask, w, 0.0)
    pv = jnp.dot(w.astype(BF16), vb, preferred_element_type=F32)
    total = st[:, tk:]
    return (total if carry is None else carry + total), (pv if acc is None else acc + pv)


def _attn_kernel(*refs, n_hist_blocks):
    if n_hist_blocks:
        q_ref, k_ref, v_ref, hk_ref, hv_ref, o_ref = refs
    else:
        q_ref, k_ref, v_ref, o_ref = refs
    tq = q_ref.shape[1]
    i = pl.program_id(2)
    new_mat = _suffix_matrix(tq)
    blk_mat = new_mat if tq == SB_KEY_BLOCK else _suffix_matrix(SB_KEY_BLOCK)
    causal = (lax.broadcasted_iota(jnp.int32, (tq, tq), 1) < lax.broadcasted_iota(jnp.int32, (tq, tq), 0))
    lane = lax.broadcasted_iota(jnp.int32, (tq, LANES), 1)
    q = q_ref[0]
    nt = (((1,), (1,)), ((), ()))

    in_head = [(lane >= h * SB_HEAD_DIM) & (lane < (h + 1) * SB_HEAD_DIM) for h in range(HEADS_PER_LANE_TILE)]
    qh = [jnp.where(m, q, jnp.zeros_like(q)) for m in in_head]

    def step(kb, vb, carries, accs, mat, mask):
        out = [_sb_step(lax.dot_general(qh[h], kb, nt, preferred_element_type=F32), vb,
                        carries[h], accs[h], mat, mask) for h in range(HEADS_PER_LANE_TILE)]
        carries, accs = tuple(c for c, _ in out), tuple(a for _, a in out)
        return carries, accs, jnp.max(functools.reduce(jnp.maximum, carries))

    def sweep(state, read_block):
        def body(s):
            kb, vb = read_block(s[0])
            return (s[0] - 1,) + step(kb, vb, s[1], s[2], blk_mat, None)
        return lax.while_loop(lambda s: (s[0] >= 0) & (s[3] > EXP_UNDERFLOW), body, state)

    def read_new(j):
        s = pl.multiple_of(j * SB_KEY_BLOCK, SB_KEY_BLOCK)
        return k_ref[0, pl.ds(s, SB_KEY_BLOCK), :], v_ref[0, pl.ds(s, SB_KEY_BLOCK), :]

    def read_hist(j):
        s = pl.multiple_of(j * SB_KEY_BLOCK, SB_KEY_BLOCK)
        return (hk_ref[0, pl.ds(s, SB_KEY_BLOCK), :].astype(BF16),
                hv_ref[0, pl.ds(s, SB_KEY_BLOCK), :].astype(BF16))

    start = pl.multiple_of(i * tq, tq)
    none = (None,) * HEADS_PER_LANE_TILE
    state = (i - 1,) + step(k_ref[0, pl.ds(start, tq), :], v_ref[0, pl.ds(start, tq), :], none, none,
                            new_mat, causal)
    if tq == SB_KEY_BLOCK:
        state = sweep(state, read_new)
    else:
        assert q_ref.shape[1] == k_ref.shape[1]
    if n_hist_blocks:
        state = sweep((jnp.int32(n_hist_blocks - 1),) + state[1:], read_hist)
    out = state[2][-1]
    for h in range(HEADS_PER_LANE_TILE - 1):
        out = jnp.where(in_head[h], state[2][h], out)
    o_ref[0] = out


def _attn(q_bf, k_bf, v_bf, hist_k=None, hist_v=None):
    n, length, _ = q_bf.shape
    tq = min(length, SB_KEY_BLOCK)
    n_tiles = D_ATTN // LANES
    qspec = pl.BlockSpec((1, tq, LANES), lambda b, t, i: (b, i, t))
    kspec = pl.BlockSpec((1, length, LANES), lambda b, t, i: (b, 0, t))
    in_specs = [qspec, kspec, kspec]
    args = [q_bf, k_bf, v_bf]
    n_hist_blocks = 0
    if hist_k is not None:
        past = hist_k.shape[1]
        assert past % SB_KEY_BLOCK == 0
        n_hist_blocks = past // SB_KEY_BLOCK
        hspec = pl.BlockSpec((1, past, LANES), lambda b, t, i: (b, 0, t))
        in_specs += [hspec, hspec]
        args += [hist_k, hist_v]
    return pl.pallas_call(
        functools.partial(_attn_kernel, n_hist_blocks=n_hist_blocks),
        grid=(n, n_tiles, length // tq),
        in_specs=in_specs,
        out_specs=qspec,
        out_shape=jax.ShapeDtypeStruct((n, length, D_ATTN), F32),
        compiler_params=_params("arbitrary", "arbitrary", "arbitrary"),
        name="attn",
    )(*args)


def _post_kernel(oa_ref, p_ref, pprev_ref, hist_ref, x_ref, mod_ref, goa_ref, gob_ref, wpool_ref,
                 pscale_ref, wout_ref, gpost1_ref, gpre2_ref, wq_ref,
                 x1_ref, h2_ref, qp_ref, xp_ref, *, offset):
    nb, tb, d = x_ref.shape
    i = pl.program_id(1)
    p = p_ref[...]
    first = jnp.broadcast_to(i == 0, (nb, POOL_TAIL, D_POOL))
    xp_ref[:, :POOL_TAIL, :] = jnp.where(first, hist_ref[...], pprev_ref[:, tb - POOL_TAIL:, :])
    xp_ref[:, POOL_TAIL:, :] = p
    pos = offset + i * tb + lax.broadcasted_iota(jnp.int32, (1, tb, 1), 1)
    mixed = []
    for g, w in enumerate(POOL_WINDOWS):
        lanes = slice(g * POOL_GROUP, (g + 1) * POOL_GROUP)
        s = p[:, :, lanes]
        for back in range(1, w):
            s = s + xp_ref[:, pl.ds(POOL_TAIL - back, tb), lanes]
        cnt = jnp.minimum(pos + 1, w).astype(F32)
        pooled = (s / cnt - p[:, :, lanes]).reshape(nb * tb, POOL_GROUP)
        mixed.append(jnp.dot(pooled.astype(BF16), wpool_ref[g], preferred_element_type=F32))
    o_b = jnp.concatenate(mixed, axis=-1) * pscale_ref[...]
    o_a = oa_ref[...].reshape(nb * tb, D_ATTN)
    cat = jnp.concatenate([_rms(o_a, goa_ref[...]), _rms(o_b, gob_ref[...])], axis=-1)
    o = jnp.dot(cat.astype(BF16), wout_ref[...], preferred_element_type=F32)
    x1 = x_ref[...] + mod_ref[:, 2:3, :] * _rms(o, gpost1_ref[...]).reshape(nb, tb, d)
    x1_ref[...] = x1
    h2 = _rms(x1, gpre2_ref[...]) * (1.0 + mod_ref[:, 4:5, :]) + mod_ref[:, 3:4, :]
    h2_ref[...] = h2
    qp = jnp.dot(h2.astype(BF16).reshape(nb * tb, d), wq_ref[...], preferred_element_type=F32)
    qp_ref[...] = qp.reshape(nb, tb, qp_ref.shape[2])


def _post(o_a, p, hist, x, mod, g_out_a, g_out_b, w_pool_bf, pool_scale, w_out_bf, g_post1, g_pre2,
          w_query_bf, offset):
    n, length, d = x.shape
    nb, tb = _token_blocking(n, length)
    assert tb >= POOL_TAIL
    dq = w_query_bf.shape[1]
    tok = lambda w: pl.BlockSpec((nb, tb, w), lambda b, i: (b, i, 0))
    full = lambda a: pl.BlockSpec(a.shape, lambda b, i: (0,) * a.ndim)
    return pl.pallas_call(
        functools.partial(_post_kernel, offset=offset),
        grid=(n // nb, length // tb),
        in_specs=[tok(D_ATTN), tok(D_POOL),
                  pl.BlockSpec((nb, tb, D_POOL), lambda b, i: (b, jnp.maximum(i - 1, 0), 0)),
                  pl.BlockSpec((nb, POOL_TAIL, D_POOL), lambda b, i: (b, 0, 0)),
                  tok(d),
                  pl.BlockSpec((nb, 6, d), lambda b, i: (b, 0, 0)),
                  full(g_out_a), full(g_out_b), full(w_pool_bf), full(pool_scale), full(w_out_bf),
                  full(g_post1), full(g_pre2), full(w_query_bf)],
        out_specs=[tok(d), tok(d), tok(dq)],
        out_shape=[jax.ShapeDtypeStruct((n, length, d), F32),
                   jax.ShapeDtypeStruct((n, length, d), F32),
                   jax.ShapeDtypeStruct((n, length, dq), F32)],
        scratch_shapes=[pltpu.VMEM((nb, tb + POOL_TAIL, D_POOL), F32)],
        compiler_params=_params("arbitrary", "arbitrary"),
        name="post",
    )(o_a, p, p, hist, x, mod, g_out_a, g_out_b, w_pool_bf, pool_scale, w_out_bf, g_post1, g_pre2,
      w_query_bf)


def _topk_rows(s, k, order):
    vals, picks = [], []
    for _ in range(k):
        m = jnp.max(s, axis=0, keepdims=True)
        pick = jnp.min(jnp.where(s == m, order, jnp.inf), axis=0, keepdims=True)
        vals.append(m)
        picks.append(pick)
        s = jnp.where(order == pick, -jnp.inf, s)
    return jnp.concatenate(vals, axis=0), jnp.concatenate(picks, axis=0).astype(jnp.int32)


def _select_rows(table, sel):
    out = jnp.zeros_like(table)
    for r in range(table.shape[0]):
        out = jnp.where(sel == r, table[r:r + 1, :], out)
    return out


def _pair_candidates(v1, v2):
    tb = v1.shape[1]
    sub = lambda n: lax.broadcasted_iota(jnp.int32, (n, tb), 0)
    vals, flat = [], []
    for a, nb in ((0, 16), (1, 8), (2, 8), (3, 8)):
        vals.append(v1[a:a + 1, :] + v2[:nb, :])
        flat.append(a * PEER_TOPK + sub(nb))
    vals.append(v1[8:16, :] + v2[0:1, :])
    flat.append((8 + sub(8)) * PEER_TOPK)
    for b in range(3):
        vals.append(jnp.where(sub(8) >= 4, v1[0:8, :] + v2[b:b + 1, :], -jnp.inf))
        flat.append(sub(8) * PEER_TOPK + b)
    return jnp.concatenate(vals, axis=0), jnp.concatenate(flat, axis=0).astype(F32)


def _route_kernel(qp_ref, keys_ref, idx_ref, gate_ref):
    nt = (((1,), (1,)), ((), ()))
    tb = qp_ref.shape[0]
    key_order = lax.broadcasted_iota(jnp.int32, (N_KEYS, tb), 0).astype(F32)
    idx_rows, gate_rows = [], []
    for h in range(PEER_HEADS):
        base = h * 2 * KEY_HALF
        q1 = qp_ref[:, base:base + KEY_HALF].astype(BF16)
        q2 = qp_ref[:, base + KEY_HALF:base + 2 * KEY_HALF].astype(BF16)
        s1 = lax.dot_general(keys_ref[0, h], q1, nt, preferred_element_type=F32)
        s2 = lax.dot_general(keys_ref[1, h], q2, nt, preferred_element_type=F32)
        v1, i1 = _topk_rows(s1, PEER_TOPK, key_order)
        v2, i2 = _topk_rows(s2, PEER_TOPK, key_order)
        cand, flat = _pair_candidates(v1, v2)
        best, sel = _topk_rows(cand, PEER_TOPK, flat)
        a_sel = lax.shift_right_logical(sel, 4)
        b_sel = sel & (PEER_TOPK - 1)
        idx_rows.append(_select_rows(i1, a_sel) * N_KEYS + _select_rows(i2, b_sel))
        e = jnp.exp(best - best[0:1, :])
        gate_rows.append(e / jnp.sum(e, axis=0, keepdims=True))
    idx_ref[...] = jnp.concatenate(idx_rows, axis=0).T
    gate_ref[...] = jnp.concatenate(gate_rows, axis=0).T


def _route(qp, sub_keys_bf):
    t, dq = qp.shape
    tb = min(t, TOKEN_BLOCK)
    assert t % tb == 0 and PEER_TOPK == 16
    return pl.pallas_call(
        _route_kernel,
        grid=(t // tb,),
        in_specs=[pl.BlockSpec((tb, dq), lambda i: (i, 0)),
                  pl.BlockSpec(sub_keys_bf.shape, lambda i: (0, 0, 0, 0))],
        out_specs=[pl.BlockSpec((tb, PEER_SEL), lambda i: (i, 0))] * 2,
        out_shape=[jax.ShapeDtypeStruct((t, PEER_SEL), jnp.int32),
                   jax.ShapeDtypeStruct((t, PEER_SEL), F32)],
        compiler_params=_params("arbitrary"),
        name="route",
    )(qp, sub_keys_bf)


def _peer_kernel(idx_ref, gate_ref, h_ref, uv_ref, o_ref, buf_ref, sem_ref):
    tg, d = h_ref.shape
    eye = (lax.broadcasted_iota(jnp.int32, (PEER_SEL, PEER_SEL), 0)
           == lax.broadcasted_iota(jnp.int32, (PEER_SEL, PEER_SEL), 1))

    def issue(t, slot):
        for j in range(PEER_SEL):
            pltpu.make_async_copy(uv_ref.at[idx_ref[t, j]], buf_ref.at[slot, pl.ds(j, 1)],
                                  sem_ref.at[slot]).start(priority=j % 2)

    def wait(slot):
        pltpu.make_async_copy(uv_ref.at[pl.ds(0, PEER_SEL), 0], buf_ref.at[slot], sem_ref.at[slot]).wait()

    def consume(t, slot):
        hrow = h_ref[pl.ds(t, 1), :]
        s_col = jnp.sum(buf_ref[slot, :, :d] * hrow, axis=1, keepdims=True)
        s_row = jnp.sum(jnp.where(eye, s_col, 0.0), axis=0, keepdims=True)
        a_row = gate_ref[pl.ds(t, 1), :] * _gelu(s_row)
        a_col = jnp.sum(jnp.where(eye, a_row, 0.0), axis=1, keepdims=True)
        o_ref[pl.ds(t, 1), :] = jnp.sum(a_col * buf_ref[slot, :, d:], axis=0, keepdims=True)

    for s in range(PEER_SLOTS):
        issue(s, s)

    def group(g, carry):
        for s in range(PEER_SLOTS):
            t = g * PEER_SLOTS + s
            wait(s)
            consume(t, s)
            issue(t + PEER_SLOTS, s)
        return carry

    n_groups = tg // PEER_SLOTS
    lax.fori_loop(0, n_groups - 1, group, 0)
    for s in range(PEER_SLOTS):
        wait(s)
        consume((n_groups - 1) * PEER_SLOTS + s, s)


def _peer(idx, gate, h2, uv):
    t, d = h2.shape
    tg = PEER_TOKEN_BLOCK
    assert t % tg == 0 and tg % PEER_SLOTS == 0 and uv.shape[1:] == (1, 2 * d)
    return pl.pallas_call(
        _peer_kernel,
        grid=(t // tg,),
        in_specs=[pl.BlockSpec((tg, PEER_SEL), lambda i: (i, 0), memory_space=pltpu.SMEM),
                  pl.BlockSpec((tg, PEER_SEL), lambda i: (i, 0)),
                  pl.BlockSpec((tg, d), lambda i: (i, 0)),
                  pl.BlockSpec(memory_space=pl.ANY)],
        out_specs=pl.BlockSpec((tg, d), lambda i: (i, 0)),
        out_shape=jax.ShapeDtypeStruct((t, d), F32),
        scratch_shapes=[pltpu.VMEM((PEER_SLOTS, PEER_SEL, 2 * d), F32),
                        pltpu.SemaphoreType.DMA((PEER_SLOTS,))],
        compiler_params=_params("arbitrary"),
        name="peer",
    )(idx, gate, h2, uv)


def _final_kernel(x1_ref, peer_ref, mod_ref, g_ref, y_ref):
    y_ref[...] = x1_ref[...] + mod_ref[:, 5:6, :] * _rms(peer_ref[...], g_ref[...])


def _final(x1, peer, mod, g_post2):
    n, length, d = x1.shape
    nb, tb = _token_blocking(n, length)
    tok = pl.BlockSpec((nb, tb, d), lambda b, i: (b, i, 0))
    return pl.pallas_call(
        _final_kernel,
        grid=(n // nb, length // tb),
        in_specs=[tok, tok, pl.BlockSpec((nb, 6, d), lambda b, i: (b, 0, 0)),
                  pl.BlockSpec((1, d), lambda b, i: (0, 0))],
        out_specs=tok,
        out_shape=jax.ShapeDtypeStruct((n, length, d), F32),
        compiler_params=_params("arbitrary", "arbitrary"),
        name="final",
    )(x1, peer, mod, g_post2)


def _layer(x, mod, hist_k, hist_v, hist_p, wts):
    (g_pre1, g_post1, g_pre2, g_post2, w_in_bf, g_out_a, g_out_b, w_pool_bf, pool_scale, w_out_bf,
     w_query_bf, sub_keys_bf, uv) = wts
    n, length, d = x.shape
    q_bf, k, v, p, k_bf, v_bf = _inproj(x, mod, g_pre1, w_in_bf)
    offset = 0 if hist_k is None else hist_k.shape[1]
    o_a = _attn(q_bf, k_bf, v_bf, hist_k, hist_v)
    x1, h2, qp = _post(o_a, p, hist_p, x, mod, g_out_a, g_out_b, w_pool_bf, pool_scale, w_out_bf,
                       g_post1, g_pre2, w_query_bf, offset)
    idx, gate = _route(qp.reshape(n * length, -1), sub_keys_bf)
    peer = _peer(idx, gate, h2.reshape(n * length, d), uv)
    y = _final(x1, peer.reshape(n, length, d), mod, g_post2)
    heads = (n, length, SB_HEADS, SB_HEAD_DIM)
    p_state = jnp.concatenate([hist_p[:, 1:], p], axis=1)[:, -POOL_HIST:]
    return y, k.reshape(heads), v.reshape(heads), p_state


def kernel(x_prompt, x_sample, cache_k, cache_v, state_pool, c_prompt, c_sample, w_ada, b_ada, g_pre1, g_post1, g_pre2, g_post2, w_in, g_out_a, g_out_b, w_pool, pool_scale, w_out, w_query, sub_keys, u_experts, v_experts):
    depth = w_ada.shape[0]
    assert depth == 1
    bp, bs = x_prompt.shape[0], x_sample.shape[0]
    row = lambda a: a.reshape(1, -1)
    c_all = jnp.concatenate([c_prompt, c_sample], axis=0)
    pad = (-c_all.shape[0]) % 8
    mod = _mod(jnp.pad(c_all, ((0, pad), (0, 0))), w_ada[0], b_ada[0]).reshape(-1, 6, D_MODEL)
    wts = (row(g_pre1[0]), row(g_post1[0]), row(g_pre2[0]), row(g_post2[0]), w_in[0].astype(BF16),
           row(g_out_a[0]), row(g_out_b[0]), w_pool[0].astype(BF16), row(pool_scale[0]),
           w_out[0].astype(BF16), w_query[0].astype(BF16), sub_keys[0].astype(BF16),
           jnp.concatenate([u_experts[0][:, None, :], v_experts[0][:, None, :]], axis=2))
    past = cache_k.shape[2]
    hist_p = jnp.pad(state_pool[0], ((0, 0), (POOL_TAIL - POOL_HIST, 0), (0, 0)))
    y_s, k_s, v_s, p_s = _layer(x_sample, mod[bp:bp + bs], cache_k[0].reshape(bs, past, D_ATTN),
                                cache_v[0].reshape(bs, past, D_ATTN), hist_p, wts)
    y_p, k_p, v_p, p_p = _layer(x_prompt, mod[:bp], None, None,
                                jnp.zeros((bp, POOL_TAIL, D_POOL), F32), wts)
    return (y_p, y_s, k_p[None], v_p[None], p_p[None], k_s[None], v_s[None], p_s[None])
```

```python
import functools

import jax
import jax.numpy as jnp
from jax import lax
from jax.experimental import pallas as pl
from jax.experimental.pallas import tpu as pltpu

F32 = jnp.float32
BF16 = jnp.bfloat16

D_MODEL = 1024
D_ATTN = D_MODEL // 2
SB_HEADS = 8
SB_HEAD_DIM = D_ATTN // SB_HEADS
D_POOL = D_MODEL - D_ATTN
POOL_WINDOWS = (2, 4, 8, 16)
POOL_GROUP = D_POOL // len(POOL_WINDOWS)
POOL_HIST = max(POOL_WINDOWS) - 1
POOL_TAIL = POOL_HIST + 1
D_IN = 3 * D_ATTN + D_POOL
N_KEYS = 128
PEER_HEADS = 8
PEER_TOPK = 16
KEY_HALF = 128
PEER_SEL = PEER_HEADS * PEER_TOPK
EPS = 1e-6

LANES = 128
HEADS_PER_LANE_TILE = LANES // SB_HEAD_DIM
SB_KEY_BLOCK = 128
SB_KEY_WINDOW = 256
SB_LANE_TILES = 2
EXP_UNDERFLOW = -104.0

TOKEN_BLOCK = 256
PEER_TOKEN_BLOCK = 256
PEER_SLOTS = 8
VMEM_LIMIT = 48 * 1024 * 1024


def _params(*sem):
    return pltpu.CompilerParams(dimension_semantics=sem, vmem_limit_bytes=VMEM_LIMIT)


def _rms(x, g):
    ms = jnp.mean(x * x, axis=-1, keepdims=True)
    return x * lax.rsqrt(ms + EPS) * g


def _gelu(x):
    return 0.5 * x * (1.0 + lax.erf(x * (2.0 ** -0.5)))


def _mod_kernel(c_ref, w_ref, b_ref, o_ref):
    s = jax.nn.silu(c_ref[...])
    o_ref[...] = jnp.dot(s, w_ref[...], precision=lax.Precision.HIGHEST,
                         preferred_element_type=F32) + b_ref[...]


def _mod(c, w_ada, b_ada):
    n, d = c.shape
    dout = w_ada.shape[1]
    return pl.pallas_call(
        _mod_kernel,
        grid=(dout // d,),
        in_specs=[pl.BlockSpec((n, d), lambda j: (0, 0)),
                  pl.BlockSpec((d, d), lambda j: (0, j)),
                  pl.BlockSpec((1, d), lambda j: (0, j))],
        out_specs=pl.BlockSpec((n, d), lambda j: (0, j)),
        out_shape=jax.ShapeDtypeStruct((n, dout), F32),
        compiler_params=_params("arbitrary"),
        name="mod",
    )(c, w_ada, b_ada.reshape(1, dout))


def _inproj_kernel(x_ref, mod_ref, g_ref, w_ref, q_ref, k_ref, v_ref, p_ref, kb_ref, vb_ref):
    nb, tb, d = x_ref.shape
    h = _rms(x_ref[...], g_ref[...]) * (1.0 + mod_ref[:, 1:2, :]) + mod_ref[:, 0:1, :]
    z = jnp.dot(h.astype(BF16).reshape(nb * tb, d), w_ref[...], preferred_element_type=F32)
    z = z.reshape(nb, tb, D_IN)
    k = z[:, :, D_ATTN:2 * D_ATTN]
    v = z[:, :, 2 * D_ATTN:3 * D_ATTN]
    q_ref[...] = (z[:, :, :D_ATTN] * (SB_HEAD_DIM ** -0.5)).astype(BF16)
    k_ref[...] = k
    v_ref[...] = v
    p_ref[...] = z[:, :, 3 * D_ATTN:]
    kb_ref[...] = k.astype(BF16)
    vb_ref[...] = v.astype(BF16)


def _token_blocking(n_batch, length):
    tb = min(length, TOKEN_BLOCK)
    nb = TOKEN_BLOCK // tb
    assert length % tb == 0 and n_batch % nb == 0 and tb % 8 == 0
    return nb, tb


def _inproj(x, mod, g_pre1, w_in_bf):
    n, length, d = x.shape
    nb, tb = _token_blocking(n, length)
    tok = lambda w: pl.BlockSpec((nb, tb, w), lambda b, i: (b, i, 0))
    out = lambda dt: jax.ShapeDtypeStruct((n, length, D_ATTN), dt)
    return pl.pallas_call(
        _inproj_kernel,
        grid=(n // nb, length // tb),
        in_specs=[tok(d),
                  pl.BlockSpec((nb, 6, d), lambda b, i: (b, 0, 0)),
                  pl.BlockSpec((1, d), lambda b, i: (0, 0)),
                  pl.BlockSpec((d, D_IN), lambda b, i: (0, 0))],
        out_specs=[tok(D_ATTN)] * 6,
        out_shape=[out(BF16), out(F32), out(F32), out(F32), out(BF16), out(BF16)],
        compiler_params=_params("arbitrary", "arbitrary"),
        name="inproj",
    )(x, mod, g_pre1, w_in_bf)


def _suffix_matrix(tk):
    shape = (2 * tk, tk + LANES)
    r = lax.broadcasted_iota(jnp.int32, shape, 0)
    r = jnp.where(r >= tk, r - tk, r)
    c = lax.broadcasted_iota(jnp.int32, shape, 1)
    return jnp.where((r > c) | (c >= tk), 1.0, 0.0).astype(BF16)


def _sb_step(z, vb, carry, suffix_mat, mask):
    tk = z.shape[1]
    sp = jnp.maximum(z, 0.0) + jnp.log1p(jnp.exp(-jnp.abs(z)))
    l = -sp if mask is None else jnp.where(mask, -sp, 0.0)
    l_hi = l.astype(BF16)
    l_lo = (l - l_hi.astype(F32)).astype(BF16)
    st = jnp.dot(jnp.concatenate([l_hi, l_lo], axis=1), suffix_mat, preferred_element_type=F32)
    e = z - sp + st[:, :tk]
    if carry is not None:
        e = e + jnp.concatenate([carry] * (tk // LANES), axis=1)
    w = jnp.exp(e)
    if mask is not None:
        w = jnp.where(mask, w, 0.0)
    pv = jnp.dot(w.astype(BF16), vb, preferred_element_type=F32)
    total = st[:, tk:]
    return (total if carry is None else carry + total), pv


def _attn_kernel(*refs, n_hist_windows):
    if n_hist_windows:
        q_ref, k_ref, v_ref, hk_ref, hv_ref, o_ref = refs
    else:
        q_ref, k_ref, v_ref, o_ref = refs
    tq = q_ref.shape[1]
    n_tiles = q_ref.shape[2] // LANES
    i = pl.program_id(2)
    nt = (((1,), (1,)), ((), ()))
    tile = lambda t: slice(t * LANES, (t + 1) * LANES)
    lane = lax.broadcasted_iota(jnp.int32, (tq, LANES), 1)
    in_head = [(lane >= h * SB_HEAD_DIM) & (lane < (h + 1) * SB_HEAD_DIM) for h in range(HEADS_PER_LANE_TILE)]
    qh = [[jnp.where(m, q_ref[0, :, tile(t)], jnp.zeros((tq, LANES), BF16)) for m in in_head]
          for t in range(n_tiles)]

    def step(read, carries, accs, mat, mask):
        new_carries, new_accs = [], []
        for t in range(n_tiles):
            kb, vb = read(t)
            pvs = []
            for h in range(HEADS_PER_LANE_TILE):
                z = lax.dot_general(qh[t][h], kb, nt, preferred_element_type=F32)
                c, pv = _sb_step(z, vb, carries[t * HEADS_PER_LANE_TILE + h], mat, mask)
                new_carries.append(c)
                pvs.append(pv)
            out = pvs[-1]
            for h in range(HEADS_PER_LANE_TILE - 1):
                out = jnp.where(in_head[h], pvs[h], out)
            new_accs.append(out if accs[t] is None else accs[t] + out)
        return tuple(new_carries), tuple(new_accs), jnp.max(functools.reduce(jnp.maximum, new_carries))

    def sweep(state, k_src, v_src, mat, partial_last):
        per = SB_KEY_WINDOW // SB_KEY_BLOCK
        col = lax.broadcasted_iota(jnp.int32, (tq, SB_KEY_WINDOW), 1)

        def body(s):
            r = s[0]
            first = jnp.maximum(r - per, 0)
            start = pl.multiple_of(first * SB_KEY_BLOCK, SB_KEY_BLOCK)
            rows = pl.ds(start, SB_KEY_WINDOW)
            mask = col < (r - first) * SB_KEY_BLOCK if partial_last else None
            read = lambda t: (k_src[0, rows, tile(t)].astype(BF16), v_src[0, rows, tile(t)].astype(BF16))
            return (r - per,) + step(read, s[1], s[2], mat, mask)
        return lax.while_loop(lambda s: (s[0] > 0) & (s[3] > EXP_UNDERFLOW), body, state)

    causal = (lax.broadcasted_iota(jnp.int32, (tq, tq), 1) < lax.broadcasted_iota(jnp.int32, (tq, tq), 0))
    rows = pl.ds(pl.multiple_of(i * tq, tq), tq)
    state = (i,) + step(lambda t: (k_ref[0, rows, tile(t)], v_ref[0, rows, tile(t)]),
                        (None,) * (n_tiles * HEADS_PER_LANE_TILE), (None,) * n_tiles, _suffix_matrix(tq), causal)
    win_mat = _suffix_matrix(SB_KEY_WINDOW)
    if tq == SB_KEY_BLOCK:
        assert k_ref.shape[1] >= SB_KEY_WINDOW
        state = sweep(state, k_ref, v_ref, win_mat, True)
    else:
        assert q_ref.shape[1] == k_ref.shape[1]
    if n_hist_windows:
        per = SB_KEY_WINDOW // SB_KEY_BLOCK
        state = sweep((jnp.int32(n_hist_windows * per),) + state[1:], hk_ref, hv_ref, win_mat, False)
    o_ref[0] = jnp.concatenate(state[2], axis=1)


def _attn(q_bf, k_bf, v_bf, hist_k=None, hist_v=None):
    n, length, _ = q_bf.shape
    tq = min(length, SB_KEY_BLOCK)
    width = SB_LANE_TILES * LANES
    qspec = pl.BlockSpec((1, tq, width), lambda b, t, i: (b, i, t))
    kspec = pl.BlockSpec((1, length, width), lambda b, t, i: (b, 0, t))
    in_specs = [qspec, kspec, kspec]
    args = [q_bf, k_bf, v_bf]
    n_hist_windows = 0
    if hist_k is not None:
        past = hist_k.shape[1]
        assert past % SB_KEY_WINDOW == 0
        n_hist_windows = past // SB_KEY_WINDOW
        hspec = pl.BlockSpec((1, past, width), lambda b, t, i: (b, 0, t))
        in_specs += [hspec, hspec]
        args += [hist_k, hist_v]
    return pl.pallas_call(
        functools.partial(_attn_kernel, n_hist_windows=n_hist_windows),
        grid=(n, D_ATTN // width, length // tq),
        in_specs=in_specs,
        out_specs=qspec,
        out_shape=jax.ShapeDtypeStruct((n, length, D_ATTN), F32),
        compiler_params=_params("arbitrary", "arbitrary", "arbitrary"),
        name="attn",
    )(*args)


def _post_kernel(oa_ref, p_ref, pprev_ref, hist_ref, x_ref, mod_ref, goa_ref, gob_ref, wpool_ref,
                 pscale_ref, wout_ref, gpost1_ref, gpre2_ref, wq_ref,
                 x1_ref, h2_ref, qp_ref, xp_ref, *, offset):
    nb, tb, d = x_ref.shape
    i = pl.program_id(1)
    p = p_ref[...]
    first = jnp.broadcast_to(i == 0, (nb, POOL_TAIL, D_POOL))
    xp_ref[:, :POOL_TAIL, :] = jnp.where(first, hist_ref[...], pprev_ref[:, tb - POOL_TAIL:, :])
    xp_ref[:, POOL_TAIL:, :] = p
    pos = offset + i * tb + lax.broadcasted_iota(jnp.int32, (1, tb, 1), 1)
    mixed = []
    for g, w in enumerate(POOL_WINDOWS):
        lanes = slice(g * POOL_GROUP, (g + 1) * POOL_GROUP)
        s = p[:, :, lanes]
        for back in range(1, w):
            s = s + xp_ref[:, pl.ds(POOL_TAIL - back, tb), lanes]
        cnt = jnp.minimum(pos + 1, w).astype(F32)
        pooled = (s / cnt - p[:, :, lanes]).reshape(nb * tb, POOL_GROUP)
        mixed.append(jnp.dot(pooled.astype(BF16), wpool_ref[g], preferred_element_type=F32))
    o_b = jnp.concatenate(mixed, axis=-1) * pscale_ref[...]
    o_a = oa_ref[...].reshape(nb * tb, D_ATTN)
    cat = jnp.concatenate([_rms(o_a, goa_ref[...]), _rms(o_b, gob_ref[...])], axis=-1)
    o = jnp.dot(cat.astype(BF16), wout_ref[...], preferred_element_type=F32)
    x1 = x_ref[...] + mod_ref[:, 2:3, :] * _rms(o, gpost1_ref[...]).reshape(nb, tb, d)
    x1_ref[...] = x1
    h2 = _rms(x1, gpre2_ref[...]) * (1.0 + mod_ref[:, 4:5, :]) + mod_ref[:, 3:4, :]
    h2_ref[...] = h2
    qp = jnp.dot(h2.astype(BF16).reshape(nb * tb, d), wq_ref[...], preferred_element_type=F32)
    qp_ref[...] = qp.reshape(nb, tb, qp_ref.shape[2])


def _post(o_a, p, hist, x, mod, g_out_a, g_out_b, w_pool_bf, pool_scale, w_out_bf, g_post1, g_pre2,
          w_query_bf, offset):
    n, length, d = x.shape
    nb, tb = _token_blocking(n, length)
    assert tb >= POOL_TAIL
    dq = w_query_bf.shape[1]
    tok = lambda w: pl.BlockSpec((nb, tb, w), lambda b, i: (b, i, 0))
    full = lambda a: pl.BlockSpec(a.shape, lambda b, i: (0,) * a.ndim)
    return pl.pallas_call(
        functools.partial(_post_kernel, offset=offset),
        grid=(n // nb, length // tb),
        in_specs=[tok(D_ATTN), tok(D_POOL),
                  pl.BlockSpec((nb, tb, D_POOL), lambda b, i: (b, jnp.maximum(i - 1, 0), 0)),
                  pl.BlockSpec((nb, POOL_TAIL, D_POOL), lambda b, i: (b, 0, 0)),
                  tok(d),
                  pl.BlockSpec((nb, 6, d), lambda b, i: (b, 0, 0)),
                  full(g_out_a), full(g_out_b), full(w_pool_bf), full(pool_scale), full(w_out_bf),
                  full(g_post1), full(g_pre2), full(w_query_bf)],
        out_specs=[tok(d), tok(d), tok(dq)],
        out_shape=[jax.ShapeDtypeStruct((n, length, d), F32),
                   jax.ShapeDtypeStruct((n, length, d), F32),
                   jax.ShapeDtypeStruct((n, length, dq), F32)],
        scratch_shapes=[pltpu.VMEM((nb, tb + POOL_TAIL, D_POOL), F32)],
        compiler_params=_params("arbitrary", "arbitrary"),
        name="post",
    )(o_a, p, p, hist, x, mod, g_out_a, g_out_b, w_pool_bf, pool_scale, w_out_bf, g_post1, g_pre2,
      w_query_bf)


def _topk_rows(s, k, order):
    vals, picks = [], []
    for _ in range(k):
        m = jnp.max(s, axis=0, keepdims=True)
        pick = jnp.min(jnp.where(s == m, order, jnp.inf), axis=0, keepdims=True)
        vals.append(m)
        picks.append(pick)
        s = jnp.where(order == pick, -jnp.inf, s)
    return jnp.concatenate(vals, axis=0), jnp.concatenate(picks, axis=0).astype(jnp.int32)


def _select_rows(table, sel):
    out = jnp.zeros_like(table)
    for r in range(table.shape[0]):
        out = jnp.where(sel == r, table[r:r + 1, :], out)
    return out


def _pair_candidates(v1, v2):
    tb = v1.shape[1]
    sub = lambda n: lax.broadcasted_iota(jnp.int32, (n, tb), 0)
    vals, flat = [], []
    for a, nb in ((0, 16), (1, 8), (2, 8), (3, 8)):
        vals.append(v1[a:a + 1, :] + v2[:nb, :])
        flat.append(a * PEER_TOPK + sub(nb))
    vals.append(v1[8:16, :] + v2[0:1, :])
    flat.append((8 + sub(8)) * PEER_TOPK)
    for b in range(3):
        vals.append(jnp.where(sub(8) >= 4, v1[0:8, :] + v2[b:b + 1, :], -jnp.inf))
        flat.append(sub(8) * PEER_TOPK + b)
    return jnp.concatenate(vals, axis=0), jnp.concatenate(flat, axis=0).astype(F32)


def _route_kernel(qp_ref, keys_ref, idx_ref, gate_ref):
    nt = (((1,), (1,)), ((), ()))
    tb = qp_ref.shape[0]
    key_order = lax.broadcasted_iota(jnp.int32, (N_KEYS, tb), 0).astype(F32)
    idx_rows, gate_rows = [], []
    for h in range(PEER_HEADS):
        base = h * 2 * KEY_HALF
        q1 = qp_ref[:, base:base + KEY_HALF].astype(BF16)
        q2 = qp_ref[:, base + KEY_HALF:base + 2 * KEY_HALF].astype(BF16)
        s1 = lax.dot_general(keys_ref[0, h], q1, nt, preferred_element_type=F32)
        s2 = lax.dot_general(keys_ref[1, h], q2, nt, preferred_element_type=F32)
        v1, i1 = _topk_rows(s1, PEER_TOPK, key_order)
        v2, i2 = _topk_rows(s2, PEER_TOPK, key_order)
        cand, flat = _pair_candidates(v1, v2)
        best, sel = _topk_rows(cand, PEER_TOPK, flat)
        a_sel = lax.shift_right_logical(sel, 4)
        b_sel = sel & (PEER_TOPK - 1)
        idx_rows.append(_select_rows(i1, a_sel) * N_KEYS + _select_rows(i2, b_sel))
        e = jnp.exp(best - best[0:1, :])
        gate_rows.append(e / jnp.sum(e, axis=0, keepdims=True))
    idx_ref[...] = jnp.concatenate(idx_rows, axis=0).T
    gate_ref[...] = jnp.concatenate(gate_rows, axis=0).T


def _route(qp, sub_keys_bf):
    t, dq = qp.shape
    tb = min(t, TOKEN_BLOCK)
    assert t % tb == 0 and PEER_TOPK == 16
    return pl.pallas_call(
        _route_kernel,
        grid=(t // tb,),
        in_specs=[pl.BlockSpec((tb, dq), lambda i: (i, 0)),
                  pl.BlockSpec(sub_keys_bf.shape, lambda i: (0, 0, 0, 0))],
        out_specs=[pl.BlockSpec((tb, PEER_SEL), lambda i: (i, 0))] * 2,
        out_shape=[jax.ShapeDtypeStruct((t, PEER_SEL), jnp.int32),
                   jax.ShapeDtypeStruct((t, PEER_SEL), F32)],
        compiler_params=_params("arbitrary"),
        name="route",
    )(qp, sub_keys_bf)


def _peer_kernel(idx_ref, gate_ref, h_ref, uv_ref, o_ref, buf_ref, sem_ref):
    tg, d = h_ref.shape
    eye = (lax.broadcasted_iota(jnp.int32, (PEER_SEL, PEER_SEL), 0)
           == lax.broadcasted_iota(jnp.int32, (PEER_SEL, PEER_SEL), 1))

    def issue(t, slot):
        for j in range(PEER_SEL):
            pltpu.make_async_copy(uv_ref.at[idx_ref[t, j]], buf_ref.at[slot, pl.ds(j, 1)],
                                  sem_ref.at[slot]).start(priority=j % 2)

    def wait(slot):
        pltpu.make_async_copy(uv_ref.at[pl.ds(0, PEER_SEL), 0], buf_ref.at[slot], sem_ref.at[slot]).wait()

    def consume(t, slot):
        hrow = h_ref[pl.ds(t, 1), :]
        s_col = jnp.sum(buf_ref[slot, :, :d] * hrow, axis=1, keepdims=True)
        s_row = jnp.sum(jnp.where(eye, s_col, 0.0), axis=0, keepdims=True)
        a_row = gate_ref[pl.ds(t, 1), :] * _gelu(s_row)
        a_col = jnp.sum(jnp.where(eye, a_row, 0.0), axis=1, keepdims=True)
        o_ref[pl.ds(t, 1), :] = jnp.sum(a_col * buf_ref[slot, :, d:], axis=0, keepdims=True)

    for s in range(PEER_SLOTS):
        issue(s, s)

    def group(g, carry):
        for s in range(PEER_SLOTS):
            t = g * PEER_SLOTS + s
            wait(s)
            consume(t, s)
            issue(t + PEER_SLOTS, s)
        return carry

    n_groups = tg // PEER_SLOTS
    lax.fori_loop(0, n_groups - 1, group, 0)
    for s in range(PEER_SLOTS):
        wait(s)
        consume((n_groups - 1) * PEER_SLOTS + s, s)


def _peer(idx, gate, h2, uv):
    t, d = h2.shape
    tg = PEER_TOKEN_BLOCK
    assert t % tg == 0 and tg % PEER_SLOTS == 0 and uv.shape[1:] == (1, 2 * d)
    return pl.pallas_call(
        _peer_kernel,
        grid=(t // tg,),
        in_specs=[pl.BlockSpec((tg, PEER_SEL), lambda i: (i, 0), memory_space=pltpu.SMEM),
                  pl.BlockSpec((tg, PEER_SEL), lambda i: (i, 0)),
                  pl.BlockSpec((tg, d), lambda i: (i, 0)),
                  pl.BlockSpec(memory_space=pl.ANY)],
        out_specs=pl.BlockSpec((tg, d), lambda i: (i, 0)),
        out_shape=jax.ShapeDtypeStruct((t, d), F32),
        scratch_shapes=[pltpu.VMEM((PEER_SLOTS, PEER_SEL, 2 * d), F32),
                        pltpu.SemaphoreType.DMA((PEER_SLOTS,))],
        compiler_params=_params("arbitrary"),
        name="peer",
    )(idx, gate, h2, uv)


def _final_kernel(x1_ref, peer_ref, mod_ref, g_ref, y_ref):
    y_ref[...] = x1_ref[...] + mod_ref[:, 5:6, :] * _rms(peer_ref[...], g_ref[...])


def _final(x1, peer, mod, g_post2):
    n, length, d = x1.shape
    nb, tb = _token_blocking(n, length)
    tok = pl.BlockSpec((nb, tb, d), lambda b, i: (b, i, 0))
    return pl.pallas_call(
        _final_kernel,
        grid=(n // nb, length // tb),
        in_specs=[tok, tok, pl.BlockSpec((nb, 6, d), lambda b, i: (b, 0, 0)),
                  pl.BlockSpec((1, d), lambda b, i: (0, 0))],
        out_specs=tok,
        out_shape=jax.ShapeDtypeStruct((n, length, d), F32),
        compiler_params=_params("arbitrary", "arbitrary"),
        name="final",
    )(x1, peer, mod, g_post2)


def _layer(x, mod, hist_k, hist_v, hist_p, wts):
    (g_pre1, g_post1, g_pre2, g_post2, w_in_bf, g_out_a, g_out_b, w_pool_bf, pool_scale, w_out_bf,
     w_query_bf, sub_keys_bf, uv) = wts
    n, length, d = x.shape
    q_bf, k, v, p, k_bf, v_bf = _inproj(x, mod, g_pre1, w_in_bf)
    offset = 0 if hist_k is None else hist_k.shape[1]
    o_a = _attn(q_bf, k_bf, v_bf, hist_k, hist_v)
    x1, h2, qp = _post(o_a, p, hist_p, x, mod, g_out_a, g_out_b, w_pool_bf, pool_scale, w_out_bf,
                       g_post1, g_pre2, w_query_bf, offset)
    idx, gate = _route(qp.reshape(n * length, -1), sub_keys_bf)
    peer = _peer(idx, gate, h2.reshape(n * length, d), uv)
    y = _final(x1, peer.reshape(n, length, d), mod, g_post2)
    heads = (n, length, SB_HEADS, SB_HEAD_DIM)
    p_state = jnp.concatenate([hist_p[:, 1:], p], axis=1)[:, -POOL_HIST:]
    return y, k.reshape(heads), v.reshape(heads), p_state


def kernel(x_prompt, x_sample, cache_k, cache_v, state_pool, c_prompt, c_sample, w_ada, b_ada, g_pre1, g_post1, g_pre2, g_post2, w_in, g_out_a, g_out_b, w_pool, pool_scale, w_out, w_query, sub_keys, u_experts, v_experts):
    depth = w_ada.shape[0]
    assert depth == 1
    bp, bs = x_prompt.shape[0], x_sample.shape[0]
    row = lambda a: a.reshape(1, -1)
    c_all = jnp.concatenate([c_prompt, c_sample], axis=0)
    pad = (-c_all.shape[0]) % 8
    mod = _mod(jnp.pad(c_all, ((0, pad), (0, 0))), w_ada[0], b_ada[0]).reshape(-1, 6, D_MODEL)
    wts = (row(g_pre1[0]), row(g_post1[0]), row(g_pre2[0]), row(g_post2[0]), w_in[0].astype(BF16),
           row(g_out_a[0]), row(g_out_b[0]), w_pool[0].astype(BF16), row(pool_scale[0]),
           w_out[0].astype(BF16), w_query[0].astype(BF16), sub_keys[0].astype(BF16),
           jnp.concatenate([u_experts[0][:, None, :], v_experts[0][:, None, :]], axis=2))
    past = cache_k.shape[2]
    hist_p = jnp.pad(state_pool[0], ((0, 0), (POOL_TAIL - POOL_HIST, 0), (0, 0)))
    y_s, k_s, v_s, p_s = _layer(x_sample, mod[bp:bp + bs], cache_k[0].reshape(bs, past, D_ATTN),
                                cache_v[0].reshape(bs, past, D_ATTN), hist_p, wts)
    y_p, k_p, v_p, p_p = _layer(x_prompt, mod[:bp], None, None,
                                jnp.zeros((bp, POOL_TAIL, D_POOL), F32), wts)
    return (y_p, y_s, k_p[None], v_p[None], p_p[None], k_s[None], v_s[None], p_s[None])
```

```python
import functools

import jax
import jax.numpy as jnp
from jax import lax
from jax.experimental import pallas as pl
from jax.experimental.pallas import tpu as pltpu

F32 = jnp.float32
BF16 = jnp.bfloat16

D_MODEL = 1024
D_ATTN = D_MODEL // 2
SB_HEADS = 8
SB_HEAD_DIM = D_ATTN // SB_HEADS
D_POOL = D_MODEL - D_ATTN
POOL_WINDOWS = (2, 4, 8, 16)
POOL_GROUP = D_POOL // len(POOL_WINDOWS)
POOL_HIST = max(POOL_WINDOWS) - 1
POOL_TAIL = POOL_HIST + 1
D_IN = 3 * D_ATTN + D_POOL
N_KEYS = 128
PEER_HEADS = 8
PEER_TOPK = 16
KEY_HALF = 128
PEER_SEL = PEER_HEADS * PEER_TOPK
EPS = 1e-6

LANES = 128
HEADS_PER_LANE_TILE = LANES // SB_HEAD_DIM
SB_KEY_BLOCK = 128
SB_KEY_WINDOW = 256
SB_LANE_TILES = 2
EXP_UNDERFLOW = -104.0

TOKEN_BLOCK = 256
PEER_TOKEN_BLOCK = 256
PEER_SLOTS = 8
VMEM_LIMIT = 48 * 1024 * 1024


def _params(*sem):
    return pltpu.CompilerParams(dimension_semantics=sem, vmem_limit_bytes=VMEM_LIMIT)


def _rms(x, g):
    ms = jnp.mean(x * x, axis=-1, keepdims=True)
    return x * lax.rsqrt(ms + EPS) * g


def _gelu(x):
    return 0.5 * x * (1.0 + lax.erf(x * (2.0 ** -0.5)))


def _mod_kernel(c_ref, w_ref, b_ref, o_ref):
    s = jax.nn.silu(c_ref[...])
    o_ref[...] = jnp.dot(s, w_ref[...], precision=lax.Precision.HIGHEST,
                         preferred_element_type=F32) + b_ref[...]


def _mod(c, w_ada, b_ada):
    n, d = c.shape
    dout = w_ada.shape[1]
    return pl.pallas_call(
        _mod_kernel,
        grid=(dout // d,),
        in_specs=[pl.BlockSpec((n, d), lambda j: (0, 0)),
                  pl.BlockSpec((d, d), lambda j: (0, j)),
                  pl.BlockSpec((1, d), lambda j: (0, j))],
        out_specs=pl.BlockSpec((n, d), lambda j: (0, j)),
        out_shape=jax.ShapeDtypeStruct((n, dout), F32),
        compiler_params=_params("arbitrary"),
        name="mod",
    )(c, w_ada, b_ada.reshape(1, dout))


def _inproj_kernel(x_ref, mod_ref, g_ref, w_ref, q_ref, k_ref, v_ref, p_ref, kb_ref, vb_ref):
    nb, tb, d = x_ref.shape
    h = _rms(x_ref[...], g_ref[...]) * (1.0 + mod_ref[:, 1:2, :]) + mod_ref[:, 0:1, :]
    z = jnp.dot(h.astype(BF16).reshape(nb * tb, d), w_ref[...], preferred_element_type=F32)
    z = z.reshape(nb, tb, D_IN)
    k = z[:, :, D_ATTN:2 * D_ATTN]
    v = z[:, :, 2 * D_ATTN:3 * D_ATTN]
    q_ref[...] = (z[:, :, :D_ATTN] * (SB_HEAD_DIM ** -0.5)).astype(BF16)
    k_ref[...] = k
    v_ref[...] = v
    p_ref[...] = z[:, :, 3 * D_ATTN:]
    kb_ref[...] = k.astype(BF16)
    vb_ref[...] = v.astype(BF16)


def _token_blocking(n_batch, length):
    tb = min(length, TOKEN_BLOCK)
    nb = TOKEN_BLOCK // tb
    assert length % tb == 0 and n_batch % nb == 0 and tb % 8 == 0
    return nb, tb


def _inproj(x, mod, g_pre1, w_in_bf):
    n, length, d = x.shape
    nb, tb = _token_blocking(n, length)
    tok = lambda w: pl.BlockSpec((nb, tb, w), lambda b, i: (b, i, 0))
    out = lambda dt: jax.ShapeDtypeStruct((n, length, D_ATTN), dt)
    return pl.pallas_call(
        _inproj_kernel,
        grid=(n // nb, length // tb),
        in_specs=[tok(d),
                  pl.BlockSpec((nb, 6, d), lambda b, i: (b, 0, 0)),
                  pl.BlockSpec((1, d), lambda b, i: (0, 0)),
                  pl.BlockSpec((d, D_IN), lambda b, i: (0, 0))],
        out_specs=[tok(D_ATTN)] * 6,
        out_shape=[out(BF16), out(F32), out(F32), out(F32), out(BF16), out(BF16)],
        compiler_params=_params("arbitrary", "arbitrary"),
        name="inproj",
    )(x, mod, g_pre1, w_in_bf)


def _suffix_matrix(tk):
    shape = (2 * tk, tk + LANES)
    r = lax.broadcasted_iota(jnp.int32, shape, 0)
    r = jnp.where(r >= tk, r - tk, r)
    c = lax.broadcasted_iota(jnp.int32, shape, 1)
    return jnp.where((r > c) | (c >= tk), 1.0, 0.0).astype(BF16)


def _sb_step(z, vb, carry, suffix_mat, mask):
    tk = z.shape[1]
    sp = jnp.maximum(z, 0.0) + jnp.log1p(jnp.exp(-jnp.abs(z)))
    l = -sp if mask is None else jnp.where(mask, -sp, 0.0)
    l_hi = l.astype(BF16)
    l_lo = (l - l_hi.astype(F32)).astype(BF16)
    st = jnp.dot(jnp.concatenate([l_hi, l_lo], axis=1), suffix_mat, preferred_element_type=F32)
    e = z - sp + st[:, :tk]
    if carry is not None:
        e = e + jnp.concatenate([carry] * (tk // LANES), axis=1)
    w = jnp.exp(e)
    if mask is not None:
        w = jnp.where(mask, w, 0.0)
    pv = jnp.dot(w.astype(BF16), vb, preferred_element_type=F32)
    total = st[:, tk:]
    return (total if carry is None else carry + total), pv


def _attn_kernel(*refs, n_hist_windows):
    if n_hist_windows:
        q_ref, k_ref, v_ref, hk_ref, hv_ref, o_ref = refs
    else:
        q_ref, k_ref, v_ref, o_ref = refs
    tq = q_ref.shape[1]
    n_tiles = q_ref.shape[2] // LANES
    i = pl.program_id(2)
    nt = (((1,), (1,)), ((), ()))
    tile = lambda t: slice(t * LANES, (t + 1) * LANES)
    lane = lax.broadcasted_iota(jnp.int32, (tq, LANES), 1)
    in_head = [(lane >= h * SB_HEAD_DIM) & (lane < (h + 1) * SB_HEAD_DIM) for h in range(HEADS_PER_LANE_TILE)]
    qh = [[jnp.where(m, q_ref[0, :, tile(t)], jnp.zeros((tq, LANES), BF16)) for m in in_head]
          for t in range(n_tiles)]

    def step(read, carries, accs, mat, mask):
        new_carries, new_accs = [], []
        for t in range(n_tiles):
            kb, vb = read(t)
            pvs = []
            for h in range(HEADS_PER_LANE_TILE):
                z = lax.dot_general(qh[t][h], kb, nt, preferred_element_type=F32)
                c, pv = _sb_step(z, vb, carries[t * HEADS_PER_LANE_TILE + h], mat, mask)
                new_carries.append(c)
                pvs.append(pv)
            out = pvs[-1]
            for h in range(HEADS_PER_LANE_TILE - 1):
                out = jnp.where(in_head[h], pvs[h], out)
            new_accs.append(out if accs[t] is None else accs[t] + out)
        return tuple(new_carries), tuple(new_accs), jnp.max(functools.reduce(jnp.maximum, new_carries))

    def sweep(state, k_src, v_src, mat, partial_last):
        per = SB_KEY_WINDOW // SB_KEY_BLOCK
        col = lax.broadcasted_iota(jnp.int32, (tq, SB_KEY_WINDOW), 1)

        def body(s):
            r = s[0]
            first = jnp.maximum(r - per, 0)
            start = pl.multiple_of(first * SB_KEY_BLOCK, SB_KEY_BLOCK)
            rows = pl.ds(start, SB_KEY_WINDOW)
            mask = col < (r - first) * SB_KEY_BLOCK if partial_last else None
            read = lambda t: (k_src[0, rows, tile(t)].astype(BF16), v_src[0, rows, tile(t)].astype(BF16))
            return (r - per,) + step(read, s[1], s[2], mat, mask)
        return lax.while_loop(lambda s: (s[0] > 0) & (s[3] > EXP_UNDERFLOW), body, state)

    causal = (lax.broadcasted_iota(jnp.int32, (tq, tq), 1) < lax.broadcasted_iota(jnp.int32, (tq, tq), 0))
    rows = pl.ds(pl.multiple_of(i * tq, tq), tq)
    state = (i,) + step(lambda t: (k_ref[0, rows, tile(t)], v_ref[0, rows, tile(t)]),
                        (None,) * (n_tiles * HEADS_PER_LANE_TILE), (None,) * n_tiles, _suffix_matrix(tq), causal)
    win_mat = _suffix_matrix(SB_KEY_WINDOW)
    if tq == SB_KEY_BLOCK:
        assert k_ref.shape[1] >= SB_KEY_WINDOW
        state = sweep(state, k_ref, v_ref, win_mat, True)
    else:
        assert q_ref.shape[1] == k_ref.shape[1]
    if n_hist_windows:
        per = SB_KEY_WINDOW // SB_KEY_BLOCK
        state = sweep((jnp.int32(n_hist_windows * per),) + state[1:], hk_ref, hv_ref, win_mat, False)
    o_ref[0] = jnp.concatenate(state[2], axis=1)


def _attn(q_bf, k_bf, v_bf, hist_k=None, hist_v=None):
    n, length, _ = q_bf.shape
    tq = min(length, SB_KEY_BLOCK)
    width = SB_LANE_TILES * LANES
    qspec = pl.BlockSpec((1, tq, width), lambda b, t, i: (b, i, t))
    kspec = pl.BlockSpec((1, length, width), lambda b, t, i: (b, 0, t))
    in_specs = [qspec, kspec, kspec]
    args = [q_bf, k_bf, v_bf]
    n_hist_windows = 0
    if hist_k is not None:
        past = hist_k.shape[1]
        assert past % SB_KEY_WINDOW == 0
        n_hist_windows = past // SB_KEY_WINDOW
        hspec = pl.BlockSpec((1, past, width), lambda b, t, i: (b, 0, t))
        in_specs += [hspec, hspec]
        args += [hist_k, hist_v]
    return pl.pallas_call(
        functools.partial(_attn_kernel, n_hist_windows=n_hist_windows),
        grid=(n, D_ATTN // width, length // tq),
        in_specs=in_specs,
        out_specs=qspec,
        out_shape=jax.ShapeDtypeStruct((n, length, D_ATTN), F32),
        compiler_params=_params("arbitrary", "arbitrary", "arbitrary"),
        name="attn",
    )(*args)


def _post_kernel(oa_ref, p_ref, pprev_ref, hist_ref, x_ref, mod_ref, goa_ref, gob_ref, wpool_ref,
                 pscale_ref, wout_ref, gpost1_ref, gpre2_ref, wq_ref,
                 x1_ref, h2_ref, qp_ref, xp_ref, *, offset):
    nb, tb, d = x_ref.shape
    i = pl.program_id(1)
    p = p_ref[...]
    first = jnp.broadcast_to(i == 0, (nb, POOL_TAIL, D_POOL))
    xp_ref[:, :POOL_TAIL, :] = jnp.where(first, hist_ref[...], pprev_ref[:, tb - POOL_TAIL:, :])
    xp_ref[:, POOL_TAIL:, :] = p
    pos = offset + i * tb + lax.broadcasted_iota(jnp.int32, (1, tb, 1), 1)
    mixed = []
    for g, w in enumerate(POOL_WINDOWS):
        lanes = slice(g * POOL_GROUP, (g + 1) * POOL_GROUP)
        s = p[:, :, lanes]
        for back in range(1, w):
            s = s + xp_ref[:, pl.ds(POOL_TAIL - back, tb), lanes]
        cnt = jnp.minimum(pos + 1, w).astype(F32)
        pooled = (s / cnt - p[:, :, lanes]).reshape(nb * tb, POOL_GROUP)
        mixed.append(jnp.dot(pooled.astype(BF16), wpool_ref[g], preferred_element_type=F32))
    o_b = jnp.concatenate(mixed, axis=-1) * pscale_ref[...]
    o_a = oa_ref[...].reshape(nb * tb, D_ATTN)
    cat = jnp.concatenate([_rms(o_a, goa_ref[...]), _rms(o_b, gob_ref[...])], axis=-1)
    o = jnp.dot(cat.astype(BF16), wout_ref[...], preferred_element_type=F32)
    x1 = x_ref[...] + mod_ref[:, 2:3, :] * _rms(o, gpost1_ref[...]).reshape(nb, tb, d)
    x1_ref[...] = x1
    h2 = _rms(x1, gpre2_ref[...]) * (1.0 + mod_ref[:, 4:5, :]) + mod_ref[:, 3:4, :]
    h2_ref[...] = h2
    qp = jnp.dot(h2.astype(BF16).reshape(nb * tb, d), wq_ref[...], preferred_element_type=F32)
    qp_ref[...] = qp.reshape(nb, tb, qp_ref.shape[2])


def _post(o_a, p, hist, x, mod, g_out_a, g_out_b, w_pool_bf, pool_scale, w_out_bf, g_post1, g_pre2,
          w_query_bf, offset):
    n, length, d = x.shape
    nb, tb = _token_blocking(n, length)
    assert tb >= POOL_TAIL
    dq = w_query_bf.shape[1]
    tok = lambda w: pl.BlockSpec((nb, tb, w), lambda b, i: (b, i, 0))
    full = lambda a: pl.BlockSpec(a.shape, lambda b, i: (0,) * a.ndim)
    return pl.pallas_call(
        functools.partial(_post_kernel, offset=offset),
        grid=(n // nb, length // tb),
        in_specs=[tok(D_ATTN), tok(D_POOL),
                  pl.BlockSpec((nb, tb, D_POOL), lambda b, i: (b, jnp.maximum(i - 1, 0), 0)),
                  pl.BlockSpec((nb, POOL_TAIL, D_POOL), lambda b, i: (b, 0, 0)),
                  tok(d),
                  pl.BlockSpec((nb, 6, d), lambda b, i: (b, 0, 0)),
                  full(g_out_a), full(g_out_b), full(w_pool_bf), full(pool_scale), full(w_out_bf),
                  full(g_post1), full(g_pre2), full(w_query_bf)],
        out_specs=[tok(d), tok(d), tok(dq)],
        out_shape=[jax.ShapeDtypeStruct((n, length, d), F32),
                   jax.ShapeDtypeStruct((n, length, d), F32),
                   jax.ShapeDtypeStruct((n, length, dq), F32)],
        scratch_shapes=[pltpu.VMEM((nb, tb + POOL_TAIL, D_POOL), F32)],
        compiler_params=_params("arbitrary", "arbitrary"),
        name="post",
    )(o_a, p, p, hist, x, mod, g_out_a, g_out_b, w_pool_bf, pool_scale, w_out_bf, g_post1, g_pre2,
      w_query_bf)


def _topk_rows(s, k, order):
    vals, picks = [], []
    for _ in range(k):
        m = jnp.max(s, axis=0, keepdims=True)
        pick = jnp.min(jnp.where(s == m, order, jnp.inf), axis=0, keepdims=True)
        vals.append(m)
        picks.append(pick)
        s = jnp.where(order == pick, -jnp.inf, s)
    return jnp.concatenate(vals, axis=0), jnp.concatenate(picks, axis=0).astype(jnp.int32)


def _select_rows(table, sel):
    out = jnp.zeros_like(table)
    for r in range(table.shape[0]):
        out = jnp.where(sel == r, table[r:r + 1, :], out)
    return out


def _pair_candidates(v1, v2):
    tb = v1.shape[1]
    sub = lambda n: lax.broadcasted_iota(jnp.int32, (n, tb), 0)
    vals, flat = [], []
    for a, nb in ((0, 16), (1, 8), (2, 8), (3, 8)):
        vals.append(v1[a:a + 1, :] + v2[:nb, :])
        flat.append(a * PEER_TOPK + sub(nb))
    vals.append(v1[8:16, :] + v2[0:1, :])
    flat.append((8 + sub(8)) * PEER_TOPK)
    for b in range(3):
        vals.append(jnp.where(sub(8) >= 4, v1[0:8, :] + v2[b:b + 1, :], -jnp.inf))
        flat.append(sub(8) * PEER_TOPK + b)
    return jnp.concatenate(vals, axis=0), jnp.concatenate(flat, axis=0).astype(F32)


def _route_kernel(qp_ref, keys_ref, idx_ref, gate_ref):
    nt = (((1,), (1,)), ((), ()))
    tb = qp_ref.shape[0]
    key_order = lax.broadcasted_iota(jnp.int32, (N_KEYS, tb), 0).astype(F32)
    idx_rows, gate_rows = [], []
    for h in range(PEER_HEADS):
        base = h * 2 * KEY_HALF
        q1 = qp_ref[:, base:base + KEY_HALF].astype(BF16)
        q2 = qp_ref[:, base + KEY_HALF:base + 2 * KEY_HALF].astype(BF16)
        s1 = lax.dot_general(keys_ref[0, h], q1, nt, preferred_element_type=F32)
        s2 = lax.dot_general(keys_ref[1, h], q2, nt, preferred_element_type=F32)
        v1, i1 = _topk_rows(s1, PEER_TOPK, key_order)
        v2, i2 = _topk_rows(s2, PEER_TOPK, key_order)
        cand, flat = _pair_candidates(v1, v2)
        best, sel = _topk_rows(cand, PEER_TOPK, flat)
        a_sel = lax.shift_right_logical(sel, 4)
        b_sel = sel & (PEER_TOPK - 1)
        idx_rows.append(_select_rows(i1, a_sel) * N_KEYS + _select_rows(i2, b_sel))
        e = jnp.exp(best - best[0:1, :])
        gate_rows.append(e / jnp.sum(e, axis=0, keepdims=True))
    idx_ref[...] = jnp.concatenate(idx_rows, axis=0).T
    gate_ref[...] = jnp.concatenate(gate_rows, axis=0).T


def _route(qp, sub_keys_bf):
    t, dq = qp.shape
    tb = min(t, TOKEN_BLOCK)
    assert t % tb == 0 and PEER_TOPK == 16
    return pl.pallas_call(
        _route_kernel,
        grid=(t // tb,),
        in_specs=[pl.BlockSpec((tb, dq), lambda i: (i, 0)),
                  pl.BlockSpec(sub_keys_bf.shape, lambda i: (0, 0, 0, 0))],
        out_specs=[pl.BlockSpec((tb, PEER_SEL), lambda i: (i, 0))] * 2,
        out_shape=[jax.ShapeDtypeStruct((t, PEER_SEL), jnp.int32),
                   jax.ShapeDtypeStruct((t, PEER_SEL), F32)],
        compiler_params=_params("arbitrary"),
        name="route",
    )(qp, sub_keys_bf)


def _peer_kernel(idx_ref, gate_ref, h_ref, uv_ref, o_ref, buf_ref, sem_ref):
    tg, d = h_ref.shape
    eye = (lax.broadcasted_iota(jnp.int32, (PEER_SEL, PEER_SEL), 0)
           == lax.broadcasted_iota(jnp.int32, (PEER_SEL, PEER_SEL), 1))

    def issue(t, slot):
        for j in range(PEER_SEL):
            pltpu.make_async_copy(uv_ref.at[idx_ref[t, j]], buf_ref.at[slot, pl.ds(j, 1)],
                                  sem_ref.at[slot]).start(priority=j % 2)

    def wait(slot):
        pltpu.make_async_copy(uv_ref.at[pl.ds(0, PEER_SEL), 0], buf_ref.at[slot], sem_ref.at[slot]).wait()

    def activations(t, slot):
        hrow = h_ref[pl.ds(t, 1), :]
        s_col = jnp.sum(buf_ref[slot, :, :d] * hrow, axis=1, keepdims=True)
        s_row = jnp.sum(jnp.where(eye, s_col, 0.0), axis=0, keepdims=True)
        return gate_ref[pl.ds(t, 1), :] * _gelu(s_row)

    def combine(t, slot, a_row):
        a_col = jnp.sum(jnp.where(eye, a_row, 0.0), axis=1, keepdims=True)
        o_ref[pl.ds(t, 1), :] = jnp.sum(a_col * buf_ref[slot, :, d:], axis=0, keepdims=True)

    for s in range(PEER_SLOTS):
        issue(s, s)
    wait(0)
    first = activations(0, 0)

    def group(g, a_row):
        for s in range(PEER_SLOTS):
            t = g * PEER_SLOTS + s
            nxt = (s + 1) % PEER_SLOTS
            wait(nxt)
            a_next = activations(t + 1, nxt)
            combine(t, s, a_row)
            issue(t + PEER_SLOTS, s)
            a_row = a_next
        return a_row

    n_groups = tg // PEER_SLOTS
    a_row = lax.fori_loop(0, n_groups - 1, group, first)
    for s in range(PEER_SLOTS):
        t = (n_groups - 1) * PEER_SLOTS + s
        if s + 1 < PEER_SLOTS:
            wait(s + 1)
            a_next = activations(t + 1, s + 1)
        combine(t, s, a_row)
        a_row = a_next


def _peer(idx, gate, h2, uv):
    t, d = h2.shape
    tg = PEER_TOKEN_BLOCK
    assert t % tg == 0 and tg % PEER_SLOTS == 0 and uv.shape[1:] == (1, 2 * d)
    return pl.pallas_call(
        _peer_kernel,
        grid=(t // tg,),
        in_specs=[pl.BlockSpec((tg, PEER_SEL), lambda i: (i, 0), memory_space=pltpu.SMEM),
                  pl.BlockSpec((tg, PEER_SEL), lambda i: (i, 0)),
                  pl.BlockSpec((tg, d), lambda i: (i, 0)),
                  pl.BlockSpec(memory_space=pl.ANY)],
        out_specs=pl.BlockSpec((tg, d), lambda i: (i, 0)),
        out_shape=jax.ShapeDtypeStruct((t, d), F32),
        scratch_shapes=[pltpu.VMEM((PEER_SLOTS, PEER_SEL, 2 * d), F32),
                        pltpu.SemaphoreType.DMA((PEER_SLOTS,))],
        compiler_params=_params("arbitrary"),
        name="peer",
    )(idx, gate, h2, uv)


def _final_kernel(x1_ref, peer_ref, mod_ref, g_ref, y_ref):
    y_ref[...] = x1_ref[...] + mod_ref[:, 5:6, :] * _rms(peer_ref[...], g_ref[...])


def _final(x1, peer, mod, g_post2):
    n, length, d = x1.shape
    nb, tb = _token_blocking(n, length)
    tok = pl.BlockSpec((nb, tb, d), lambda b, i: (b, i, 0))
    return pl.pallas_call(
        _final_kernel,
        grid=(n // nb, length // tb),
        in_specs=[tok, tok, pl.BlockSpec((nb, 6, d), lambda b, i: (b, 0, 0)),
                  pl.BlockSpec((1, d), lambda b, i: (0, 0))],
        out_specs=tok,
        out_shape=jax.ShapeDtypeStruct((n, length, d), F32),
        compiler_params=_params("arbitrary", "arbitrary"),
        name="final",
    )(x1, peer, mod, g_post2)


def _layer(x, mod, hist_k, hist_v, hist_p, wts):
    (g_pre1, g_post1, g_pre2, g_post2, w_in_bf, g_out_a, g_out_b, w_pool_bf, pool_scale, w_out_bf,
     w_query_bf, sub_keys_bf, uv) = wts
    n, length, d = x.shape
    q_bf, k, v, p, k_bf, v_bf = _inproj(x, mod, g_pre1, w_in_bf)
    offset = 0 if hist_k is None else hist_k.shape[1]
    o_a = _attn(q_bf, k_bf, v_bf, hist_k, hist_v)
    x1, h2, qp = _post(o_a, p, hist_p, x, mod, g_out_a, g_out_b, w_pool_bf, pool_scale, w_out_bf,
                       g_post1, g_pre2, w_query_bf, offset)
    idx, gate = _route(qp.reshape(n * length, -1), sub_keys_bf)
    peer = _peer(idx, gate, h2.reshape(n * length, d), uv)
    y = _final(x1, peer.reshape(n, length, d), mod, g_post2)
    heads = (n, length, SB_HEADS, SB_HEAD_DIM)
    p_state = jnp.concatenate([hist_p[:, 1:], p], axis=1)[:, -POOL_HIST:]
    return y, k.reshape(heads), v.reshape(heads), p_state


def kernel(x_prompt, x_sample, cache_k, cache_v, state_pool, c_prompt, c_sample, w_ada, b_ada, g_pre1, g_post1, g_pre2, g_post2, w_in, g_out_a, g_out_b, w_pool, pool_scale, w_out, w_query, sub_keys, u_experts, v_experts):
    depth = w_ada.shape[0]
    assert depth == 1
    bp, bs = x_prompt.shape[0], x_sample.shape[0]
    row = lambda a: a.reshape(1, -1)
    c_all = jnp.concatenate([c_prompt, c_sample], axis=0)
    pad = (-c_all.shape[0]) % 8
    mod = _mod(jnp.pad(c_all, ((0, pad), (0, 0))), w_ada[0], b_ada[0]).reshape(-1, 6, D_MODEL)
    wts = (row(g_pre1[0]), row(g_post1[0]), row(g_pre2[0]), row(g_post2[0]), w_in[0].astype(BF16),
           row(g_out_a[0]), row(g_out_b[0]), w_pool[0].astype(BF16), row(pool_scale[0]),
           w_out[0].astype(BF16), w_query[0].astype(BF16), sub_keys[0].astype(BF16),
           jnp.concatenate([u_experts[0][:, None, :], v_experts[0][:, None, :]], axis=2))
    past = cache_k.shape[2]
    hist_p = jnp.pad(state_pool[0], ((0, 0), (POOL_TAIL - POOL_HIST, 0), (0, 0)))
    y_s, k_s, v_s, p_s = _layer(x_sample, mod[bp:bp + bs], cache_k[0].reshape(bs, past, D_ATTN),
                                cache_v[0].reshape(bs, past, D_ATTN), hist_p, wts)
    y_p, k_p, v_p, p_p = _layer(x_prompt, mod[:bp], None, None,
                                jnp.zeros((bp, POOL_TAIL, D_POOL), F32), wts)
    return (y_p, y_s, k_p[None], v_p[None], p_p[None], k_s[None], v_s[None], p_s[None])
```

```python
import functools

import jax
import jax.numpy as jnp
from jax import lax
from jax.experimental import pallas as pl
from jax.experimental.pallas import tpu as pltpu

F32 = jnp.float32
BF16 = jnp.bfloat16

D_MODEL = 1024
D_ATTN = D_MODEL // 2
SB_HEADS = 8
SB_HEAD_DIM = D_ATTN // SB_HEADS
D_POOL = D_MODEL - D_ATTN
POOL_WINDOWS = (2, 4, 8, 16)
POOL_GROUP = D_POOL // len(POOL_WINDOWS)
POOL_HIST = max(POOL_WINDOWS) - 1
POOL_TAIL = POOL_HIST + 1
D_IN = 3 * D_ATTN + D_POOL
N_KEYS = 128
PEER_HEADS = 8
PEER_TOPK = 16
KEY_HALF = 128
PEER_SEL = PEER_HEADS * PEER_TOPK
EPS = 1e-6

LANES = 128
HEADS_PER_LANE_TILE = LANES // SB_HEAD_DIM
SB_KEY_BLOCK = 128
SB_KEY_WINDOW = 256
SB_LANE_TILES = 2
EXP_UNDERFLOW = -104.0

TOKEN_BLOCK = 256
PEER_TOKEN_BLOCK = 256
PEER_SLOTS = 8
VMEM_LIMIT = 48 * 1024 * 1024


def _params(*sem):
    return pltpu.CompilerParams(dimension_semantics=sem, vmem_limit_bytes=VMEM_LIMIT)


def _rms(x, g):
    ms = jnp.mean(x * x, axis=-1, keepdims=True)
    return x * lax.rsqrt(ms + EPS) * g


def _gelu(x):
    return 0.5 * x * (1.0 + lax.erf(x * (2.0 ** -0.5)))


def _mod_kernel(c_ref, w_ref, b_ref, o_ref):
    s = jax.nn.silu(c_ref[...])
    o_ref[...] = jnp.dot(s, w_ref[...], precision=lax.Precision.HIGHEST,
                         preferred_element_type=F32) + b_ref[...]


def _mod(c, w_ada, b_ada):
    n, d = c.shape
    dout = w_ada.shape[1]
    return pl.pallas_call(
        _mod_kernel,
        grid=(dout // d,),
        in_specs=[pl.BlockSpec((n, d), lambda j: (0, 0)),
                  pl.BlockSpec((d, d), lambda j: (0, j)),
                  pl.BlockSpec((1, d), lambda j: (0, j))],
        out_specs=pl.BlockSpec((n, d), lambda j: (0, j)),
        out_shape=jax.ShapeDtypeStruct((n, dout), F32),
        compiler_params=_params("arbitrary"),
        name="mod",
    )(c, w_ada, b_ada.reshape(1, dout))


def _inproj_kernel(x_ref, mod_ref, g_ref, w_ref, q_ref, k_ref, v_ref, p_ref, kb_ref, vb_ref):
    nb, tb, d = x_ref.shape
    h = _rms(x_ref[...], g_ref[...]) * (1.0 + mod_ref[:, 1:2, :]) + mod_ref[:, 0:1, :]
    z = jnp.dot(h.astype(BF16).reshape(nb * tb, d), w_ref[...], preferred_element_type=F32)
    z = z.reshape(nb, tb, D_IN)
    k = z[:, :, D_ATTN:2 * D_ATTN]
    v = z[:, :, 2 * D_ATTN:3 * D_ATTN]
    q_ref[...] = (z[:, :, :D_ATTN] * (SB_HEAD_DIM ** -0.5)).astype(BF16)
    k_ref[...] = k
    v_ref[...] = v
    p_ref[...] = z[:, :, 3 * D_ATTN:]
    kb_ref[...] = k.astype(BF16)
    vb_ref[...] = v.astype(BF16)


def _token_blocking(n_batch, length):
    tb = min(length, TOKEN_BLOCK)
    nb = TOKEN_BLOCK // tb
    assert length % tb == 0 and n_batch % nb == 0 and tb % 8 == 0
    return nb, tb


def _inproj(x, mod, g_pre1, w_in_bf):
    n, length, d = x.shape
    nb, tb = _token_blocking(n, length)
    tok = lambda w: pl.BlockSpec((nb, tb, w), lambda b, i: (b, i, 0))
    out = lambda dt: jax.ShapeDtypeStruct((n, length, D_ATTN), dt)
    return pl.pallas_call(
        _inproj_kernel,
        grid=(n // nb, length // tb),
        in_specs=[tok(d),
                  pl.BlockSpec((nb, 6, d), lambda b, i: (b, 0, 0)),
                  pl.BlockSpec((1, d), lambda b, i: (0, 0)),
                  pl.BlockSpec((d, D_IN), lambda b, i: (0, 0))],
        out_specs=[tok(D_ATTN)] * 6,
        out_shape=[out(BF16), out(F32), out(F32), out(F32), out(BF16), out(BF16)],
        compiler_params=_params("arbitrary", "arbitrary"),
        name="inproj",
    )(x, mod, g_pre1, w_in_bf)


def _suffix_matrix(tk):
    shape = (2 * tk, tk + LANES)
    r = lax.broadcasted_iota(jnp.int32, shape, 0)
    r = jnp.where(r >= tk, r - tk, r)
    c = lax.broadcasted_iota(jnp.int32, shape, 1)
    return jnp.where((r > c) | (c >= tk), 1.0, 0.0).astype(BF16)


def _sb_step(z, vb, carry, suffix_mat, mask):
    tk = z.shape[1]
    sp = jnp.maximum(z, 0.0) + jnp.log1p(jnp.exp(-jnp.abs(z)))
    l = -sp if mask is None else jnp.where(mask, -sp, 0.0)
    l_hi = l.astype(BF16)
    l_lo = (l - l_hi.astype(F32)).astype(BF16)
    st = jnp.dot(jnp.concatenate([l_hi, l_lo], axis=1), suffix_mat, preferred_element_type=F32)
    e = z - sp + st[:, :tk]
    if carry is not None:
        e = e + jnp.concatenate([carry] * (tk // LANES), axis=1)
    w = jnp.exp(e)
    if mask is not None:
        w = jnp.where(mask, w, 0.0)
    pv = jnp.dot(w.astype(BF16), vb, preferred_element_type=F32)
    total = st[:, tk:]
    return (total if carry is None else carry + total), pv


def _attn_kernel(*refs, n_hist_windows):
    if n_hist_windows:
        q_ref, k_ref, v_ref, hk_ref, hv_ref, o_ref = refs
    else:
        q_ref, k_ref, v_ref, o_ref = refs
    tq = q_ref.shape[1]
    n_tiles = q_ref.shape[2] // LANES
    i = pl.program_id(2)
    nt = (((1,), (1,)), ((), ()))
    tile = lambda t: slice(t * LANES, (t + 1) * LANES)
    lane = lax.broadcasted_iota(jnp.int32, (tq, LANES), 1)
    in_head = [(lane >= h * SB_HEAD_DIM) & (lane < (h + 1) * SB_HEAD_DIM) for h in range(HEADS_PER_LANE_TILE)]
    qh = [[jnp.where(m, q_ref[0, :, tile(t)], jnp.zeros((tq, LANES), BF16)) for m in in_head]
          for t in range(n_tiles)]

    def step(read, carries, accs, mat, mask):
        new_carries, new_accs = [], []
        for t in range(n_tiles):
            kb, vb = read(t)
            pvs = []
            for h in range(HEADS_PER_LANE_TILE):
                z = lax.dot_general(qh[t][h], kb, nt, preferred_element_type=F32)
                c, pv = _sb_step(z, vb, carries[t * HEADS_PER_LANE_TILE + h], mat, mask)
                new_carries.append(c)
                pvs.append(pv)
            out = pvs[-1]
            for h in range(HEADS_PER_LANE_TILE - 1):
                out = jnp.where(in_head[h], pvs[h], out)
            new_accs.append(out if accs[t] is None else accs[t] + out)
        return tuple(new_carries), tuple(new_accs), jnp.max(functools.reduce(jnp.maximum, new_carries))

    def sweep(state, k_src, v_src, mat, partial_last):
        per = SB_KEY_WINDOW // SB_KEY_BLOCK
        col = lax.broadcasted_iota(jnp.int32, (tq, SB_KEY_WINDOW), 1)

        def body(s):
            r = s[0]
            first = jnp.maximum(r - per, 0)
            start = pl.multiple_of(first * SB_KEY_BLOCK, SB_KEY_BLOCK)
            rows = pl.ds(start, SB_KEY_WINDOW)
            mask = col < (r - first) * SB_KEY_BLOCK if partial_last else None
            read = lambda t: (k_src[0, rows, tile(t)].astype(BF16), v_src[0, rows, tile(t)].astype(BF16))
            return (r - per,) + step(read, s[1], s[2], mat, mask)
        return lax.while_loop(lambda s: (s[0] > 0) & (s[3] > EXP_UNDERFLOW), body, state)

    causal = (lax.broadcasted_iota(jnp.int32, (tq, tq), 1) < lax.broadcasted_iota(jnp.int32, (tq, tq), 0))
    rows = pl.ds(pl.multiple_of(i * tq, tq), tq)
    state = (i,) + step(lambda t: (k_ref[0, rows, tile(t)], v_ref[0, rows, tile(t)]),
                        (None,) * (n_tiles * HEADS_PER_LANE_TILE), (None,) * n_tiles, _suffix_matrix(tq), causal)
    win_mat = _suffix_matrix(SB_KEY_WINDOW)
    if tq == SB_KEY_BLOCK:
        assert k_ref.shape[1] >= SB_KEY_WINDOW
        state = sweep(state, k_ref, v_ref, win_mat, True)
    else:
        assert q_ref.shape[1] == k_ref.shape[1]
    if n_hist_windows:
        per = SB_KEY_WINDOW // SB_KEY_BLOCK
        state = sweep((jnp.int32(n_hist_windows * per),) + state[1:], hk_ref, hv_ref, win_mat, False)
    o_ref[0] = jnp.concatenate(state[2], axis=1)


def _attn(q_bf, k_bf, v_bf, hist_k=None, hist_v=None):
    n, length, _ = q_bf.shape
    tq = min(length, SB_KEY_BLOCK)
    width = SB_LANE_TILES * LANES
    qspec = pl.BlockSpec((1, tq, width), lambda b, t, i: (b, i, t))
    kspec = pl.BlockSpec((1, length, width), lambda b, t, i: (b, 0, t))
    in_specs = [qspec, kspec, kspec]
    args = [q_bf, k_bf, v_bf]
    n_hist_windows = 0
    if hist_k is not None:
        past = hist_k.shape[1]
        assert past % SB_KEY_WINDOW == 0
        n_hist_windows = past // SB_KEY_WINDOW
        hspec = pl.BlockSpec((1, past, width), lambda b, t, i: (b, 0, t))
        in_specs += [hspec, hspec]
        args += [hist_k, hist_v]
    return pl.pallas_call(
        functools.partial(_attn_kernel, n_hist_windows=n_hist_windows),
        grid=(n, D_ATTN // width, length // tq),
        in_specs=in_specs,
        out_specs=qspec,
        out_shape=jax.ShapeDtypeStruct((n, length, D_ATTN), F32),
        compiler_params=_params("arbitrary", "arbitrary", "arbitrary"),
        name="attn",
    )(*args)


def _post_kernel(oa_ref, p_ref, pprev_ref, hist_ref, x_ref, mod_ref, goa_ref, gob_ref, wpool_ref,
                 pscale_ref, wout_ref, gpost1_ref, gpre2_ref, wq_ref,
                 x1_ref, h2_ref, qp_ref, xp_ref, *, offset):
    nb, tb, d = x_ref.shape
    i = pl.program_id(1)
    p = p_ref[...]
    first = jnp.broadcast_to(i == 0, (nb, POOL_TAIL, D_POOL))
    xp_ref[:, :POOL_TAIL, :] = jnp.where(first, hist_ref[...], pprev_ref[:, tb - POOL_TAIL:, :])
    xp_ref[:, POOL_TAIL:, :] = p
    pos = offset + i * tb + lax.broadcasted_iota(jnp.int32, (1, tb, 1), 1)
    mixed = []
    for g, w in enumerate(POOL_WINDOWS):
        lanes = slice(g * POOL_GROUP, (g + 1) * POOL_GROUP)
        s = p[:, :, lanes]
        for back in range(1, w):
            s = s + xp_ref[:, pl.ds(POOL_TAIL - back, tb), lanes]
        cnt = jnp.minimum(pos + 1, w).astype(F32)
        pooled = (s / cnt - p[:, :, lanes]).reshape(nb * tb, POOL_GROUP)
        mixed.append(jnp.dot(pooled.astype(BF16), wpool_ref[g], preferred_element_type=F32))
    o_b = jnp.concatenate(mixed, axis=-1) * pscale_ref[...]
    o_a = oa_ref[...].reshape(nb * tb, D_ATTN)
    cat = jnp.concatenate([_rms(o_a, goa_ref[...]), _rms(o_b, gob_ref[...])], axis=-1)
    o = jnp.dot(cat.astype(BF16), wout_ref[...], preferred_element_type=F32)
    x1 = x_ref[...] + mod_ref[:, 2:3, :] * _rms(o, gpost1_ref[...]).reshape(nb, tb, d)
    x1_ref[...] = x1
    h2 = _rms(x1, gpre2_ref[...]) * (1.0 + mod_ref[:, 4:5, :]) + mod_ref[:, 3:4, :]
    h2_ref[...] = h2
    qp = jnp.dot(h2.astype(BF16).reshape(nb * tb, d), wq_ref[...], preferred_element_type=F32)
    qp_ref[...] = qp.reshape(nb, tb, qp_ref.shape[2])


def _post(o_a, p, hist, x, mod, g_out_a, g_out_b, w_pool_bf, pool_scale, w_out_bf, g_post1, g_pre2,
          w_query_bf, offset):
    n, length, d = x.shape
    nb, tb = _token_blocking(n, length)
    assert tb >= POOL_TAIL
    dq = w_query_bf.shape[1]
    tok = lambda w: pl.BlockSpec((nb, tb, w), lambda b, i: (b, i, 0))
    full = lambda a: pl.BlockSpec(a.shape, lambda b, i: (0,) * a.ndim)
    return pl.pallas_call(
        functools.partial(_post_kernel, offset=offset),
        grid=(n // nb, length // tb),
        in_specs=[tok(D_ATTN), tok(D_POOL),
                  pl.BlockSpec((nb, tb, D_POOL), lambda b, i: (b, jnp.maximum(i - 1, 0), 0)),
                  pl.BlockSpec((nb, POOL_TAIL, D_POOL), lambda b, i: (b, 0, 0)),
                  tok(d),
                  pl.BlockSpec((nb, 6, d), lambda b, i: (b, 0, 0)),
                  full(g_out_a), full(g_out_b), full(w_pool_bf), full(pool_scale), full(w_out_bf),
                  full(g_post1), full(g_pre2), full(w_query_bf)],
        out_specs=[tok(d), tok(d), tok(dq)],
        out_shape=[jax.ShapeDtypeStruct((n, length, d), F32),
                   jax.ShapeDtypeStruct((n, length, d), F32),
                   jax.ShapeDtypeStruct((n, length, dq), F32)],
        scratch_shapes=[pltpu.VMEM((nb, tb + POOL_TAIL, D_POOL), F32)],
        compiler_params=_params("arbitrary", "arbitrary"),
        name="post",
    )(o_a, p, p, hist, x, mod, g_out_a, g_out_b, w_pool_bf, pool_scale, w_out_bf, g_post1, g_pre2,
      w_query_bf)


def _topk_rows(s, k, order):
    vals, picks = [], []
    for _ in range(k):
        m = jnp.max(s, axis=0, keepdims=True)
        pick = jnp.min(jnp.where(s == m, order, jnp.inf), axis=0, keepdims=True)
        vals.append(m)
        picks.append(pick)
        s = jnp.where(order == pick, -jnp.inf, s)
    return jnp.concatenate(vals, axis=0), jnp.concatenate(picks, axis=0).astype(jnp.int32)


def _select_rows(table, sel):
    out = jnp.zeros_like(table)
    for r in range(table.shape[0]):
        out = jnp.where(sel == r, table[r:r + 1, :], out)
    return out


def _pair_candidates(v1, v2):
    tb = v1.shape[1]
    sub = lambda n: lax.broadcasted_iota(jnp.int32, (n, tb), 0)
    vals, flat = [], []
    for a, nb in ((0, 16), (1, 8), (2, 8), (3, 8)):
        vals.append(v1[a:a + 1, :] + v2[:nb, :])
        flat.append(a * PEER_TOPK + sub(nb))
    vals.append(v1[8:16, :] + v2[0:1, :])
    flat.append((8 + sub(8)) * PEER_TOPK)
    for b in range(3):
        vals.append(jnp.where(sub(8) >= 4, v1[0:8, :] + v2[b:b + 1, :], -jnp.inf))
        flat.append(sub(8) * PEER_TOPK + b)
    return jnp.concatenate(vals, axis=0), jnp.concatenate(flat, axis=0).astype(F32)


def _route_kernel(qp_ref, keys_ref, idx_ref, gate_ref):
    nt = (((1,), (1,)), ((), ()))
    tb = qp_ref.shape[0]
    key_order = lax.broadcasted_iota(jnp.int32, (N_KEYS, tb), 0).astype(F32)
    idx_rows, gate_rows = [], []
    for h in range(PEER_HEADS):
        base = h * 2 * KEY_HALF
        q1 = qp_ref[:, base:base + KEY_HALF].astype(BF16)
        q2 = qp_ref[:, base + KEY_HALF:base + 2 * KEY_HALF].astype(BF16)
        s1 = lax.dot_general(keys_ref[0, h], q1, nt, preferred_element_type=F32)
        s2 = lax.dot_general(keys_ref[1, h], q2, nt, preferred_element_type=F32)
        v1, i1 = _topk_rows(s1, PEER_TOPK, key_order)
        v2, i2 = _topk_rows(s2, PEER_TOPK, key_order)
        cand, flat = _pair_candidates(v1, v2)
        best, sel = _topk_rows(cand, PEER_TOPK, flat)
        a_sel = lax.shift_right_logical(sel, 4)
        b_sel = sel & (PEER_TOPK - 1)
        idx_rows.append(_select_rows(i1, a_sel) * N_KEYS + _select_rows(i2, b_sel))
        e = jnp.exp(best - best[0:1, :])
        gate_rows.append(e / jnp.sum(e, axis=0, keepdims=True))
    idx_ref[...] = jnp.concatenate(idx_rows, axis=0).T
    gate_ref[...] = jnp.concatenate(gate_rows, axis=0).T


def _route(qp, sub_keys_bf):
    t, dq = qp.shape
    tb = min(t, TOKEN_BLOCK)
    assert t % tb == 0 and PEER_TOPK == 16
    return pl.pallas_call(
        _route_kernel,
        grid=(t // tb,),
        in_specs=[pl.BlockSpec((tb, dq), lambda i: (i, 0)),
                  pl.BlockSpec(sub_keys_bf.shape, lambda i: (0, 0, 0, 0))],
        out_specs=[pl.BlockSpec((tb, PEER_SEL), lambda i: (i, 0))] * 2,
        out_shape=[jax.ShapeDtypeStruct((t, PEER_SEL), jnp.int32),
                   jax.ShapeDtypeStruct((t, PEER_SEL), F32)],
        compiler_params=_params("arbitrary"),
        name="route",
    )(qp, sub_keys_bf)


def _peer_kernel(idx_ref, gate_ref, h_ref, uv_ref, o_ref, buf_ref, sem_ref):
    tg, d = h_ref.shape
    eye = (lax.broadcasted_iota(jnp.int32, (PEER_SEL, PEER_SEL), 0)
           == lax.broadcasted_iota(jnp.int32, (PEER_SEL, PEER_SEL), 1))

    def issue(t, slot):
        for j in range(PEER_SEL):
            pltpu.make_async_copy(uv_ref.at[idx_ref[t, j]], buf_ref.at[slot, pl.ds(j, 1)],
                                  sem_ref.at[slot]).start(priority=j % 2)

    def wait(slot):
        pltpu.make_async_copy(uv_ref.at[pl.ds(0, PEER_SEL), 0], buf_ref.at[slot], sem_ref.at[slot]).wait()

    def activations(t, slot):
        hrow = h_ref[pl.ds(t, 1), :]
        u = lax.bitcast_convert_type(buf_ref[slot] & jnp.uint32(0xFFFF0000), F32)
        s_col = jnp.sum(u * hrow, axis=1, keepdims=True)
        s_row = jnp.sum(jnp.where(eye, s_col, 0.0), axis=0, keepdims=True)
        return gate_ref[pl.ds(t, 1), :] * _gelu(s_row)

    def combine(t, slot, a_row):
        a_col = jnp.sum(jnp.where(eye, a_row, 0.0), axis=1, keepdims=True)
        v = lax.bitcast_convert_type(buf_ref[slot] << 16, F32)
        o_ref[pl.ds(t, 1), :] = jnp.sum(a_col * v, axis=0, keepdims=True)

    for s in range(PEER_SLOTS):
        issue(s, s)
    wait(0)
    first = activations(0, 0)

    def group(g, a_row):
        for s in range(PEER_SLOTS):
            t = g * PEER_SLOTS + s
            nxt = (s + 1) % PEER_SLOTS
            wait(nxt)
            a_next = activations(t + 1, nxt)
            combine(t, s, a_row)
            issue(t + PEER_SLOTS, s)
            a_row = a_next
        return a_row

    n_groups = tg // PEER_SLOTS
    a_row = lax.fori_loop(0, n_groups - 1, group, first)
    for s in range(PEER_SLOTS):
        t = (n_groups - 1) * PEER_SLOTS + s
        if s + 1 < PEER_SLOTS:
            wait(s + 1)
            a_next = activations(t + 1, s + 1)
        combine(t, s, a_row)
        a_row = a_next


def _peer(idx, gate, h2, uv):
    t, d = h2.shape
    tg = PEER_TOKEN_BLOCK
    assert t % tg == 0 and tg % PEER_SLOTS == 0 and uv.shape[1:] == (1, d) and uv.dtype == jnp.uint32
    return pl.pallas_call(
        _peer_kernel,
        grid=(t // tg,),
        in_specs=[pl.BlockSpec((tg, PEER_SEL), lambda i: (i, 0), memory_space=pltpu.SMEM),
                  pl.BlockSpec((tg, PEER_SEL), lambda i: (i, 0)),
                  pl.BlockSpec((tg, d), lambda i: (i, 0)),
                  pl.BlockSpec(memory_space=pl.ANY)],
        out_specs=pl.BlockSpec((tg, d), lambda i: (i, 0)),
        out_shape=jax.ShapeDtypeStruct((t, d), F32),
        scratch_shapes=[pltpu.VMEM((PEER_SLOTS, PEER_SEL, d), jnp.uint32),
                        pltpu.SemaphoreType.DMA((PEER_SLOTS,))],
        compiler_params=_params("arbitrary"),
        name="peer",
    )(idx, gate, h2, uv)


def _final_kernel(x1_ref, peer_ref, mod_ref, g_ref, y_ref):
    y_ref[...] = x1_ref[...] + mod_ref[:, 5:6, :] * _rms(peer_ref[...], g_ref[...])


def _final(x1, peer, mod, g_post2):
    n, length, d = x1.shape
    nb, tb = _token_blocking(n, length)
    tok = pl.BlockSpec((nb, tb, d), lambda b, i: (b, i, 0))
    return pl.pallas_call(
        _final_kernel,
        grid=(n // nb, length // tb),
        in_specs=[tok, tok, pl.BlockSpec((nb, 6, d), lambda b, i: (b, 0, 0)),
                  pl.BlockSpec((1, d), lambda b, i: (0, 0))],
        out_specs=tok,
        out_shape=jax.ShapeDtypeStruct((n, length, d), F32),
        compiler_params=_params("arbitrary", "arbitrary"),
        name="final",
    )(x1, peer, mod, g_post2)


def _layer(x, mod, hist_k, hist_v, hist_p, wts):
    (g_pre1, g_post1, g_pre2, g_post2, w_in_bf, g_out_a, g_out_b, w_pool_bf, pool_scale, w_out_bf,
     w_query_bf, sub_keys_bf, uv) = wts
    n, length, d = x.shape
    q_bf, k, v, p, k_bf, v_bf = _inproj(x, mod, g_pre1, w_in_bf)
    offset = 0 if hist_k is None else hist_k.shape[1]
    o_a = _attn(q_bf, k_bf, v_bf, hist_k, hist_v)
    x1, h2, qp = _post(o_a, p, hist_p, x, mod, g_out_a, g_out_b, w_pool_bf, pool_scale, w_out_bf,
                       g_post1, g_pre2, w_query_bf, offset)
    idx, gate = _route(qp.reshape(n * length, -1), sub_keys_bf)
    peer = _peer(idx, gate, h2.reshape(n * length, d), uv)
    y = _final(x1, peer.reshape(n, length, d), mod, g_post2)
    heads = (n, length, SB_HEADS, SB_HEAD_DIM)
    p_state = jnp.concatenate([hist_p[:, 1:], p], axis=1)[:, -POOL_HIST:]
    return y, k.reshape(heads), v.reshape(heads), p_state


def _pack_experts(u, v):
    half = lambda a: lax.bitcast_convert_type(a.astype(jnp.bfloat16), jnp.uint16).astype(jnp.uint32)
    return ((half(u) << 16) | half(v))[:, None, :]


def kernel(x_prompt, x_sample, cache_k, cache_v, state_pool, c_prompt, c_sample, w_ada, b_ada, g_pre1, g_post1, g_pre2, g_post2, w_in, g_out_a, g_out_b, w_pool, pool_scale, w_out, w_query, sub_keys, u_experts, v_experts):
    depth = w_ada.shape[0]
    assert depth == 1
    bp, bs = x_prompt.shape[0], x_sample.shape[0]
    row = lambda a: a.reshape(1, -1)
    c_all = jnp.concatenate([c_prompt, c_sample], axis=0)
    pad = (-c_all.shape[0]) % 8
    mod = _mod(jnp.pad(c_all, ((0, pad), (0, 0))), w_ada[0], b_ada[0]).reshape(-1, 6, D_MODEL)
    wts = (row(g_pre1[0]), row(g_post1[0]), row(g_pre2[0]), row(g_post2[0]), w_in[0].astype(BF16),
           row(g_out_a[0]), row(g_out_b[0]), w_pool[0].astype(BF16), row(pool_scale[0]),
           w_out[0].astype(BF16), w_query[0].astype(BF16), sub_keys[0].astype(BF16),
           _pack_experts(u_experts[0], v_experts[0]))
    past = cache_k.shape[2]
    hist_p = jnp.pad(state_pool[0], ((0, 0), (POOL_TAIL - POOL_HIST, 0), (0, 0)))
    y_s, k_s, v_s, p_s = _layer(x_sample, mod[bp:bp + bs], cache_k[0].reshape(bs, past, D_ATTN),
                                cache_v[0].reshape(bs, past, D_ATTN), hist_p, wts)
    y_p, k_p, v_p, p_p = _layer(x_prompt, mod[:bp], None, None,
                                jnp.zeros((bp, POOL_TAIL, D_POOL), F32), wts)
    return (y_p, y_s, k_p[None], v_p[None], p_p[None], k_s[None], v_s[None], p_s[None])
```

```python
import functools

import jax
import jax.numpy as jnp
from jax import lax
from jax.experimental import pallas as pl
from jax.experimental.pallas import tpu as pltpu

F32 = jnp.float32
BF16 = jnp.bfloat16

D_MODEL = 1024
D_ATTN = D_MODEL // 2
SB_HEADS = 8
SB_HEAD_DIM = D_ATTN // SB_HEADS
D_POOL = D_MODEL - D_ATTN
POOL_WINDOWS = (2, 4, 8, 16)
POOL_GROUP = D_POOL // len(POOL_WINDOWS)
POOL_HIST = max(POOL_WINDOWS) - 1
POOL_TAIL = POOL_HIST + 1
D_IN = 3 * D_ATTN + D_POOL
N_KEYS = 128
PEER_HEADS = 8
PEER_TOPK = 16
KEY_HALF = 128
PEER_SEL = PEER_HEADS * PEER_TOPK
EPS = 1e-6

LANES = 128
HEADS_PER_LANE_TILE = LANES // SB_HEAD_DIM
SB_KEY_BLOCK = 128
SB_KEY_WINDOW = 256
SB_LANE_TILES = 2
EXP_UNDERFLOW = -104.0

TOKEN_BLOCK = 256
PEER_TOKEN_BLOCK = 256
PEER_SLOTS = 8
PEER_ROUTE_RUNS = 4
VMEM_LIMIT = 48 * 1024 * 1024


def _params(*sem):
    return pltpu.CompilerParams(dimension_semantics=sem, vmem_limit_bytes=VMEM_LIMIT)


def _rms(x, g):
    ms = jnp.mean(x * x, axis=-1, keepdims=True)
    return x * lax.rsqrt(ms + EPS) * g


def _gelu(x):
    return 0.5 * x * (1.0 + lax.erf(x * (2.0 ** -0.5)))


def _mod_kernel(c_ref, w_ref, b_ref, o_ref):
    s = jax.nn.silu(c_ref[...])
    o_ref[...] = jnp.dot(s, w_ref[...], precision=lax.Precision.HIGHEST,
                         preferred_element_type=F32) + b_ref[...]


def _mod(c, w_ada, b_ada):
    n, d = c.shape
    dout = w_ada.shape[1]
    return pl.pallas_call(
        _mod_kernel,
        grid=(dout // d,),
        in_specs=[pl.BlockSpec((n, d), lambda j: (0, 0)),
                  pl.BlockSpec((d, d), lambda j: (0, j)),
                  pl.BlockSpec((1, d), lambda j: (0, j))],
        out_specs=pl.BlockSpec((n, d), lambda j: (0, j)),
        out_shape=jax.ShapeDtypeStruct((n, dout), F32),
        compiler_params=_params("arbitrary"),
        name="mod",
    )(c, w_ada, b_ada.reshape(1, dout))


def _inproj_kernel(x_ref, mod_ref, g_ref, w_ref, q_ref, k_ref, v_ref, p_ref, kb_ref, vb_ref):
    nb, tb, d = x_ref.shape
    h = _rms(x_ref[...], g_ref[...]) * (1.0 + mod_ref[:, 1:2, :]) + mod_ref[:, 0:1, :]
    z = jnp.dot(h.astype(BF16).reshape(nb * tb, d), w_ref[...], preferred_element_type=F32)
    z = z.reshape(nb, tb, D_IN)
    k = z[:, :, D_ATTN:2 * D_ATTN]
    v = z[:, :, 2 * D_ATTN:3 * D_ATTN]
    q_ref[...] = (z[:, :, :D_ATTN] * (SB_HEAD_DIM ** -0.5)).astype(BF16)
    k_ref[...] = k
    v_ref[...] = v
    p_ref[...] = z[:, :, 3 * D_ATTN:]
    kb_ref[...] = k.astype(BF16)
    vb_ref[...] = v.astype(BF16)


def _token_blocking(n_batch, length):
    tb = min(length, TOKEN_BLOCK)
    nb = TOKEN_BLOCK // tb
    assert length % tb == 0 and n_batch % nb == 0 and tb % 8 == 0
    return nb, tb


def _inproj(x, mod, g_pre1, w_in_bf):
    n, length, d = x.shape
    nb, tb = _token_blocking(n, length)
    tok = lambda w: pl.BlockSpec((nb, tb, w), lambda b, i: (b, i, 0))
    out = lambda dt: jax.ShapeDtypeStruct((n, length, D_ATTN), dt)
    return pl.pallas_call(
        _inproj_kernel,
        grid=(n // nb, length // tb),
        in_specs=[tok(d),
                  pl.BlockSpec((nb, 6, d), lambda b, i: (b, 0, 0)),
                  pl.BlockSpec((1, d), lambda b, i: (0, 0)),
                  pl.BlockSpec((d, D_IN), lambda b, i: (0, 0))],
        out_specs=[tok(D_ATTN)] * 6,
        out_shape=[out(BF16), out(F32), out(F32), out(F32), out(BF16), out(BF16)],
        compiler_params=_params("arbitrary", "arbitrary"),
        name="inproj",
    )(x, mod, g_pre1, w_in_bf)


def _suffix_matrix(tk):
    shape = (2 * tk, tk + LANES)
    r = lax.broadcasted_iota(jnp.int32, shape, 0)
    r = jnp.where(r >= tk, r - tk, r)
    c = lax.broadcasted_iota(jnp.int32, shape, 1)
    return jnp.where((r > c) | (c >= tk), 1.0, 0.0).astype(BF16)


def _sb_step(z, vb, carry, suffix_mat, mask):
    tk = z.shape[1]
    sp = jnp.maximum(z, 0.0) + jnp.log1p(jnp.exp(-jnp.abs(z)))
    l = -sp if mask is None else jnp.where(mask, -sp, 0.0)
    l_hi = l.astype(BF16)
    l_lo = (l - l_hi.astype(F32)).astype(BF16)
    st = jnp.dot(jnp.concatenate([l_hi, l_lo], axis=1), suffix_mat, preferred_element_type=F32)
    e = z - sp + st[:, :tk]
    if carry is not None:
        e = e + jnp.concatenate([carry] * (tk // LANES), axis=1)
    w = jnp.exp(e)
    if mask is not None:
        w = jnp.where(mask, w, 0.0)
    pv = jnp.dot(w.astype(BF16), vb, preferred_element_type=F32)
    total = st[:, tk:]
    return (total if carry is None else carry + total), pv


def _attn_kernel(*refs, n_hist_windows):
    if n_hist_windows:
        q_ref, k_ref, v_ref, hk_ref, hv_ref, o_ref = refs
    else:
        q_ref, k_ref, v_ref, o_ref = refs
    tq = q_ref.shape[1]
    n_tiles = q_ref.shape[2] // LANES
    i = pl.program_id(2)
    nt = (((1,), (1,)), ((), ()))
    tile = lambda t: slice(t * LANES, (t + 1) * LANES)
    lane = lax.broadcasted_iota(jnp.int32, (tq, LANES), 1)
    in_head = [(lane >= h * SB_HEAD_DIM) & (lane < (h + 1) * SB_HEAD_DIM) for h in range(HEADS_PER_LANE_TILE)]
    qh = [[jnp.where(m, q_ref[0, :, tile(t)], jnp.zeros((tq, LANES), BF16)) for m in in_head]
          for t in range(n_tiles)]

    def step(read, carries, accs, mat, mask):
        new_carries, new_accs = [], []
        for t in range(n_tiles):
            kb, vb = read(t)
            pvs = []
            for h in range(HEADS_PER_LANE_TILE):
                z = lax.dot_general(qh[t][h], kb, nt, preferred_element_type=F32)
                c, pv = _sb_step(z, vb, carries[t * HEADS_PER_LANE_TILE + h], mat, mask)
                new_carries.append(c)
                pvs.append(pv)
            out = pvs[-1]
            for h in range(HEADS_PER_LANE_TILE - 1):
                out = jnp.where(in_head[h], pvs[h], out)
            new_accs.append(out if accs[t] is None else accs[t] + out)
        return tuple(new_carries), tuple(new_accs), jnp.max(functools.reduce(jnp.maximum, new_carries))

    def sweep(state, k_src, v_src, mat, partial_last):
        per = SB_KEY_WINDOW // SB_KEY_BLOCK
        col = lax.broadcasted_iota(jnp.int32, (tq, SB_KEY_WINDOW), 1)

        def body(s):
            r = s[0]
            first = jnp.maximum(r - per, 0)
            start = pl.multiple_of(first * SB_KEY_BLOCK, SB_KEY_BLOCK)
            rows = pl.ds(start, SB_KEY_WINDOW)
            mask = col < (r - first) * SB_KEY_BLOCK if partial_last else None
            read = lambda t: (k_src[0, rows, tile(t)].astype(BF16), v_src[0, rows, tile(t)].astype(BF16))
            return (r - per,) + step(read, s[1], s[2], mat, mask)
        return lax.while_loop(lambda s: (s[0] > 0) & (s[3] > EXP_UNDERFLOW), body, state)

    causal = (lax.broadcasted_iota(jnp.int32, (tq, tq), 1) < lax.broadcasted_iota(jnp.int32, (tq, tq), 0))
    rows = pl.ds(pl.multiple_of(i * tq, tq), tq)
    state = (i,) + step(lambda t: (k_ref[0, rows, tile(t)], v_ref[0, rows, tile(t)]),
                        (None,) * (n_tiles * HEADS_PER_LANE_TILE), (None,) * n_tiles, _suffix_matrix(tq), causal)
    win_mat = _suffix_matrix(SB_KEY_WINDOW)
    if tq == SB_KEY_BLOCK:
        assert k_ref.shape[1] >= SB_KEY_WINDOW
        state = sweep(state, k_ref, v_ref, win_mat, True)
    else:
        assert q_ref.shape[1] == k_ref.shape[1]
    if n_hist_windows:
        per = SB_KEY_WINDOW // SB_KEY_BLOCK
        state = sweep((jnp.int32(n_hist_windows * per),) + state[1:], hk_ref, hv_ref, win_mat, False)
    o_ref[0] = jnp.concatenate(state[2], axis=1)


def _attn(q_bf, k_bf, v_bf, hist_k=None, hist_v=None):
    n, length, _ = q_bf.shape
    tq = min(length, SB_KEY_BLOCK)
    width = SB_LANE_TILES * LANES
    qspec = pl.BlockSpec((1, tq, width), lambda b, t, i: (b, i, t))
    kspec = pl.BlockSpec((1, length, width), lambda b, t, i: (b, 0, t))
    in_specs = [qspec, kspec, kspec]
    args = [q_bf, k_bf, v_bf]
    n_hist_windows = 0
    if hist_k is not None:
        past = hist_k.shape[1]
        assert past % SB_KEY_WINDOW == 0
        n_hist_windows = past // SB_KEY_WINDOW
        hspec = pl.BlockSpec((1, past, width), lambda b, t, i: (b, 0, t))
        in_specs += [hspec, hspec]
        args += [hist_k, hist_v]
    return pl.pallas_call(
        functools.partial(_attn_kernel, n_hist_windows=n_hist_windows),
        grid=(n, D_ATTN // width, length // tq),
        in_specs=in_specs,
        out_specs=qspec,
        out_shape=jax.ShapeDtypeStruct((n, length, D_ATTN), F32),
        compiler_params=_params("arbitrary", "arbitrary", "arbitrary"),
        name="attn",
    )(*args)


def _post_kernel(oa_ref, p_ref, pprev_ref, hist_ref, x_ref, mod_ref, goa_ref, gob_ref, wpool_ref,
                 pscale_ref, wout_ref, gpost1_ref, gpre2_ref, wq_ref,
                 x1_ref, h2_ref, qp_ref, xp_ref, *, offset):
    nb, tb, d = x_ref.shape
    i = pl.program_id(1)
    p = p_ref[...]
    first = jnp.broadcast_to(i == 0, (nb, POOL_TAIL, D_POOL))
    xp_ref[:, :POOL_TAIL, :] = jnp.where(first, hist_ref[...], pprev_ref[:, tb - POOL_TAIL:, :])
    xp_ref[:, POOL_TAIL:, :] = p
    pos = offset + i * tb + lax.broadcasted_iota(jnp.int32, (1, tb, 1), 1)
    mixed = []
    for g, w in enumerate(POOL_WINDOWS):
        lanes = slice(g * POOL_GROUP, (g + 1) * POOL_GROUP)
        s = p[:, :, lanes]
        for back in range(1, w):
            s = s + xp_ref[:, pl.ds(POOL_TAIL - back, tb), lanes]
        cnt = jnp.minimum(pos + 1, w).astype(F32)
        pooled = (s / cnt - p[:, :, lanes]).reshape(nb * tb, POOL_GROUP)
        mixed.append(jnp.dot(pooled.astype(BF16), wpool_ref[g], preferred_element_type=F32))
    o_b = jnp.concatenate(mixed, axis=-1) * pscale_ref[...]
    o_a = oa_ref[...].reshape(nb * tb, D_ATTN)
    cat = jnp.concatenate([_rms(o_a, goa_ref[...]), _rms(o_b, gob_ref[...])], axis=-1)
    o = jnp.dot(cat.astype(BF16), wout_ref[...], preferred_element_type=F32)
    x1 = x_ref[...] + mod_ref[:, 2:3, :] * _rms(o, gpost1_ref[...]).reshape(nb, tb, d)
    x1_ref[...] = x1
    h2 = _rms(x1, gpre2_ref[...]) * (1.0 + mod_ref[:, 4:5, :]) + mod_ref[:, 3:4, :]
    h2_ref[...] = h2
    qp = jnp.dot(h2.astype(BF16).reshape(nb * tb, d), wq_ref[...], preferred_element_type=F32)
    qp_ref[...] = qp.reshape(nb, tb, qp_ref.shape[2])


def _post(o_a, p, hist, x, mod, g_out_a, g_out_b, w_pool_bf, pool_scale, w_out_bf, g_post1, g_pre2,
          w_query_bf, offset):
    n, length, d = x.shape
    nb, tb = _token_blocking(n, length)
    assert tb >= POOL_TAIL
    dq = w_query_bf.shape[1]
    tok = lambda w: pl.BlockSpec((nb, tb, w), lambda b, i: (b, i, 0))
    full = lambda a: pl.BlockSpec(a.shape, lambda b, i: (0,) * a.ndim)
    return pl.pallas_call(
        functools.partial(_post_kernel, offset=offset),
        grid=(n // nb, length // tb),
        in_specs=[tok(D_ATTN), tok(D_POOL),
                  pl.BlockSpec((nb, tb, D_POOL), lambda b, i: (b, jnp.maximum(i - 1, 0), 0)),
                  pl.BlockSpec((nb, POOL_TAIL, D_POOL), lambda b, i: (b, 0, 0)),
                  tok(d),
                  pl.BlockSpec((nb, 6, d), lambda b, i: (b, 0, 0)),
                  full(g_out_a), full(g_out_b), full(w_pool_bf), full(pool_scale), full(w_out_bf),
                  full(g_post1), full(g_pre2), full(w_query_bf)],
        out_specs=[tok(d), tok(d), tok(dq)],
        out_shape=[jax.ShapeDtypeStruct((n, length, d), F32),
                   jax.ShapeDtypeStruct((n, length, d), F32),
                   jax.ShapeDtypeStruct((n, length, dq), F32)],
        scratch_shapes=[pltpu.VMEM((nb, tb + POOL_TAIL, D_POOL), F32)],
        compiler_params=_params("arbitrary", "arbitrary"),
        name="post",
    )(o_a, p, p, hist, x, mod, g_out_a, g_out_b, w_pool_bf, pool_scale, w_out_bf, g_post1, g_pre2,
      w_query_bf)


def _topk_rows(s, k, order):
    vals, picks = [], []
    for _ in range(k):
        m = jnp.max(s, axis=0, keepdims=True)
        pick = jnp.min(jnp.where(s == m, order, jnp.inf), axis=0, keepdims=True)
        vals.append(m)
        picks.append(pick)
        s = jnp.where(order == pick, -jnp.inf, s)
    return jnp.concatenate(vals, axis=0), jnp.concatenate(picks, axis=0).astype(jnp.int32)


def _select_rows(table, sel):
    out = jnp.zeros_like(table)
    for r in range(table.shape[0]):
        out = jnp.where(sel == r, table[r:r + 1, :], out)
    return out


def _pair_candidates(v1, v2):
    tb = v1.shape[1]
    sub = lambda n: lax.broadcasted_iota(jnp.int32, (n, tb), 0)
    vals, flat = [], []
    for a, nb in ((0, 16), (1, 8), (2, 8), (3, 8)):
        vals.append(v1[a:a + 1, :] + v2[:nb, :])
        flat.append(a * PEER_TOPK + sub(nb))
    vals.append(v1[8:16, :] + v2[0:1, :])
    flat.append((8 + sub(8)) * PEER_TOPK)
    for b in range(3):
        vals.append(jnp.where(sub(8) >= 4, v1[0:8, :] + v2[b:b + 1, :], -jnp.inf))
        flat.append(sub(8) * PEER_TOPK + b)
    return jnp.concatenate(vals, axis=0), jnp.concatenate(flat, axis=0).astype(F32)


def _route_head(qp_ref, keys_ref, h, key_order):
    nt = (((1,), (1,)), ((), ()))
    base = h * 2 * KEY_HALF
    q1 = qp_ref[:, base:base + KEY_HALF].astype(BF16)
    q2 = qp_ref[:, base + KEY_HALF:base + 2 * KEY_HALF].astype(BF16)
    s1 = lax.dot_general(keys_ref[0, h], q1, nt, preferred_element_type=F32)
    s2 = lax.dot_general(keys_ref[1, h], q2, nt, preferred_element_type=F32)
    v1, i1 = _topk_rows(s1, PEER_TOPK, key_order)
    v2, i2 = _topk_rows(s2, PEER_TOPK, key_order)
    cand, flat = _pair_candidates(v1, v2)
    best, sel = _topk_rows(cand, PEER_TOPK, flat)
    a_sel = lax.shift_right_logical(sel, 4)
    b_sel = sel & (PEER_TOPK - 1)
    idx = _select_rows(i1, a_sel) * N_KEYS + _select_rows(i2, b_sel)
    e = jnp.exp(best - best[0:1, :])
    return idx, e / jnp.sum(e, axis=0, keepdims=True)


def _peer_kernel(qp_first_ref, qp_next_ref, keys_ref, h_ref, uv_ref, o_ref,
                 buf_ref, idx_ref, gate_ref, idx_t_ref, gate_t_ref, idx_stage_ref, sem_ref, idx_sem_ref):
    tg, d = h_ref.shape
    i = pl.program_id(0)
    cur = lax.rem(i, 2)
    eye = (lax.broadcasted_iota(jnp.int32, (PEER_SEL, PEER_SEL), 0)
           == lax.broadcasted_iota(jnp.int32, (PEER_SEL, PEER_SEL), 1))
    key_order = lax.broadcasted_iota(jnp.int32, (N_KEYS, tg), 0).astype(F32)

    def route_heads(qp_ref, heads):
        for h in heads:
            idx, gate = _route_head(qp_ref, keys_ref, h, key_order)
            idx_t_ref[h * PEER_TOPK:(h + 1) * PEER_TOPK, :] = idx
            gate_t_ref[h * PEER_TOPK:(h + 1) * PEER_TOPK, :] = gate

    def publish(parity):
        gate_ref[parity] = gate_t_ref[...].T
        idx_stage_ref[...] = idx_t_ref[...].T
        copy = pltpu.make_async_copy(idx_stage_ref, idx_ref.at[parity], idx_sem_ref.at[0])
        copy.start()
        copy.wait()

    @pl.when(i == 0)
    def _():
        route_heads(qp_first_ref, range(PEER_HEADS))
        publish(0)

    def issue(t, slot):
        for j in range(PEER_SEL):
            pltpu.make_async_copy(uv_ref.at[idx_ref[cur, t, j]], buf_ref.at[slot, pl.ds(j, 1)],
                                  sem_ref.at[slot]).start(priority=j % 2)

    def wait(slot):
        pltpu.make_async_copy(uv_ref.at[pl.ds(0, PEER_SEL), 0], buf_ref.at[slot], sem_ref.at[slot]).wait()

    def activations(t, slot):
        hrow = h_ref[pl.ds(t, 1), :]
        s_col = jnp.sum(buf_ref[slot, :, :d] * hrow, axis=1, keepdims=True)
        s_row = jnp.sum(jnp.where(eye, s_col, 0.0), axis=0, keepdims=True)
        return gate_ref[cur, pl.ds(t, 1), :] * _gelu(s_row)

    def combine(t, slot, a_row):
        a_col = jnp.sum(jnp.where(eye, a_row, 0.0), axis=1, keepdims=True)
        o_ref[pl.ds(t, 1), :] = jnp.sum(a_col * buf_ref[slot, :, d:], axis=0, keepdims=True)

    for s in range(PEER_SLOTS):
        issue(s, s)
    wait(0)
    a_row = activations(0, 0)

    def group(g, a_row):
        for s in range(PEER_SLOTS):
            t = g * PEER_SLOTS + s
            nxt = (s + 1) % PEER_SLOTS
            wait(nxt)
            a_next = activations(t + 1, nxt)
            combine(t, s, a_row)
            issue(t + PEER_SLOTS, s)
            a_row = a_next
        return a_row

    n_groups = tg // PEER_SLOTS
    heads_per_run = PEER_HEADS // PEER_ROUTE_RUNS
    for r in range(PEER_ROUTE_RUNS):
        lo, hi = (n_groups - 1) * r // PEER_ROUTE_RUNS, (n_groups - 1) * (r + 1) // PEER_ROUTE_RUNS
        a_row = lax.fori_loop(lo, hi, group, a_row)
        route_heads(qp_next_ref, range(r * heads_per_run, (r + 1) * heads_per_run))
    for s in range(PEER_SLOTS):
        t = (n_groups - 1) * PEER_SLOTS + s
        if s + 1 < PEER_SLOTS:
            wait(s + 1)
            a_next = activations(t + 1, s + 1)
        combine(t, s, a_row)
        a_row = a_next
    publish(1 - cur)


def _peer(qp, sub_keys_bf, h2, uv):
    t, d = h2.shape
    dq = qp.shape[1]
    tg = PEER_TOKEN_BLOCK
    n_blocks = t // tg
    assert t % tg == 0 and tg % PEER_SLOTS == 0 and uv.shape[1:] == (1, 2 * d)
    assert PEER_HEADS % PEER_ROUTE_RUNS == 0 and PEER_TOPK == 16
    return pl.pallas_call(
        _peer_kernel,
        grid=(n_blocks,),
        in_specs=[pl.BlockSpec((tg, dq), lambda i: (0, 0)),
                  pl.BlockSpec((tg, dq), lambda i: (jnp.minimum(i + 1, n_blocks - 1), 0)),
                  pl.BlockSpec(sub_keys_bf.shape, lambda i: (0, 0, 0, 0)),
                  pl.BlockSpec((tg, d), lambda i: (i, 0)),
                  pl.BlockSpec(memory_space=pl.ANY)],
        out_specs=pl.BlockSpec((tg, d), lambda i: (i, 0)),
        out_shape=jax.ShapeDtypeStruct((t, d), F32),
        scratch_shapes=[pltpu.VMEM((PEER_SLOTS, PEER_SEL, 2 * d), F32),
                        pltpu.SMEM((2, tg, PEER_SEL), jnp.int32),
                        pltpu.VMEM((2, tg, PEER_SEL), F32),
                        pltpu.VMEM((PEER_SEL, tg), jnp.int32),
                        pltpu.VMEM((PEER_SEL, tg), F32),
                        pltpu.VMEM((tg, PEER_SEL), jnp.int32),
                        pltpu.SemaphoreType.DMA((PEER_SLOTS,)),
                        pltpu.SemaphoreType.DMA((1,))],
        compiler_params=_params("arbitrary"),
        name="peer",
    )(qp, qp, sub_keys_bf, h2, uv)


def _final_kernel(x1_ref, peer_ref, mod_ref, g_ref, y_ref):
    y_ref[...] = x1_ref[...] + mod_ref[:, 5:6, :] * _rms(peer_ref[...], g_ref[...])


def _final(x1, peer, mod, g_post2):
    n, length, d = x1.shape
    nb, tb = _token_blocking(n, length)
    tok = pl.BlockSpec((nb, tb, d), lambda b, i: (b, i, 0))
    return pl.pallas_call(
        _final_kernel,
        grid=(n // nb, length // tb),
        in_specs=[tok, tok, pl.BlockSpec((nb, 6, d), lambda b, i: (b, 0, 0)),
                  pl.BlockSpec((1, d), lambda b, i: (0, 0))],
        out_specs=tok,
        out_shape=jax.ShapeDtypeStruct((n, length, d), F32),
        compiler_params=_params("arbitrary", "arbitrary"),
        name="final",
    )(x1, peer, mod, g_post2)


def _layer(x, mod, hist_k, hist_v, hist_p, wts):
    (g_pre1, g_post1, g_pre2, g_post2, w_in_bf, g_out_a, g_out_b, w_pool_bf, pool_scale, w_out_bf,
     w_query_bf, sub_keys_bf, uv) = wts
    n, length, d = x.shape
    q_bf, k, v, p, k_bf, v_bf = _inproj(x, mod, g_pre1, w_in_bf)
    offset = 0 if hist_k is None else hist_k.shape[1]
    o_a = _attn(q_bf, k_bf, v_bf, hist_k, hist_v)
    x1, h2, qp = _post(o_a, p, hist_p, x, mod, g_out_a, g_out_b, w_pool_bf, pool_scale, w_out_bf,
                       g_post1, g_pre2, w_query_bf, offset)
    peer = _peer(qp.reshape(n * length, -1), sub_keys_bf, h2.reshape(n * length, d), uv)
    y = _final(x1, peer.reshape(n, length, d), mod, g_post2)
    heads = (n, length, SB_HEADS, SB_HEAD_DIM)
    p_state = jnp.concatenate([hist_p[:, 1:], p], axis=1)[:, -POOL_HIST:]
    return y, k.reshape(heads), v.reshape(heads), p_state


def kernel(x_prompt, x_sample, cache_k, cache_v, state_pool, c_prompt, c_sample, w_ada, b_ada, g_pre1, g_post1, g_pre2, g_post2, w_in, g_out_a, g_out_b, w_pool, pool_scale, w_out, w_query, sub_keys, u_experts, v_experts):
    depth = w_ada.shape[0]
    assert depth == 1
    bp, bs = x_prompt.shape[0], x_sample.shape[0]
    row = lambda a: a.reshape(1, -1)
    c_all = jnp.concatenate([c_prompt, c_sample], axis=0)
    pad = (-c_all.shape[0]) % 8
    mod = _mod(jnp.pad(c_all, ((0, pad), (0, 0))), w_ada[0], b_ada[0]).reshape(-1, 6, D_MODEL)
    wts = (row(g_pre1[0]), row(g_post1[0]), row(g_pre2[0]), row(g_post2[0]), w_in[0].astype(BF16),
           row(g_out_a[0]), row(g_out_b[0]), w_pool[0].astype(BF16), row(pool_scale[0]),
           w_out[0].astype(BF16), w_query[0].astype(BF16), sub_keys[0].astype(BF16),
           jnp.concatenate([u_experts[0][:, None, :], v_experts[0][:, None, :]], axis=2))
    past = cache_k.shape[2]
    hist_p = jnp.pad(state_pool[0], ((0, 0), (POOL_TAIL - POOL_HIST, 0), (0, 0)))
    y_s, k_s, v_s, p_s = _layer(x_sample, mod[bp:bp + bs], cache_k[0].reshape(bs, past, D_ATTN),
                                cache_v[0].reshape(bs, past, D_ATTN), hist_p, wts)
    y_p, k_p, v_p, p_p = _layer(x_prompt, mod[:bp], None, None,
                                jnp.zeros((bp, POOL_TAIL, D_POOL), F32), wts)
    return (y_p, y_s, k_p[None], v_p[None], p_p[None], k_s[None], v_s[None], p_s[None])
```

```python
import functools

import jax
import jax.numpy as jnp
from jax import lax
from jax.experimental import pallas as pl
from jax.experimental.pallas import tpu as pltpu

F32 = jnp.float32
BF16 = jnp.bfloat16

D_MODEL = 1024
D_ATTN = D_MODEL // 2
SB_HEADS = 8
SB_HEAD_DIM = D_ATTN // SB_HEADS
D_POOL = D_MODEL - D_ATTN
POOL_WINDOWS = (2, 4, 8, 16)
POOL_GROUP = D_POOL // len(POOL_WINDOWS)
POOL_HIST = max(POOL_WINDOWS) - 1
POOL_TAIL = POOL_HIST + 1
D_IN = 3 * D_ATTN + D_POOL
N_KEYS = 128
PEER_HEADS = 8
PEER_TOPK = 16
KEY_HALF = 128
PEER_SEL = PEER_HEADS * PEER_TOPK
EPS = 1e-6

LANES = 128
HEADS_PER_LANE_TILE = LANES // SB_HEAD_DIM
SB_KEY_BLOCK = 128
SB_KEY_WINDOW = 256
SB_LANE_TILES = 2
EXP_UNDERFLOW = -104.0

TOKEN_BLOCK = 256
PEER_TOKEN_BLOCK = 256
PEER_SLOTS = 16
VMEM_LIMIT = 48 * 1024 * 1024


def _params(*sem):
    return pltpu.CompilerParams(dimension_semantics=sem, vmem_limit_bytes=VMEM_LIMIT)


def _rms(x, g):
    ms = jnp.mean(x * x, axis=-1, keepdims=True)
    return x * lax.rsqrt(ms + EPS) * g


def _gelu(x):
    return 0.5 * x * (1.0 + lax.erf(x * (2.0 ** -0.5)))


def _mod_kernel(c_ref, w_ref, b_ref, o_ref):
    s = jax.nn.silu(c_ref[...])
    o_ref[...] = jnp.dot(s, w_ref[...], precision=lax.Precision.HIGHEST,
                         preferred_element_type=F32) + b_ref[...]


def _mod(c, w_ada, b_ada):
    n, d = c.shape
    dout = w_ada.shape[1]
    return pl.pallas_call(
        _mod_kernel,
        grid=(dout // d,),
        in_specs=[pl.BlockSpec((n, d), lambda j: (0, 0)),
                  pl.BlockSpec((d, d), lambda j: (0, j)),
                  pl.BlockSpec((1, d), lambda j: (0, j))],
        out_specs=pl.BlockSpec((n, d), lambda j: (0, j)),
        out_shape=jax.ShapeDtypeStruct((n, dout), F32),
        compiler_params=_params("arbitrary"),
        name="mod",
    )(c, w_ada, b_ada.reshape(1, dout))


def _inproj_kernel(x_ref, mod_ref, g_ref, w_ref, q_ref, k_ref, v_ref, p_ref, kb_ref, vb_ref):
    nb, tb, d = x_ref.shape
    h = _rms(x_ref[...], g_ref[...]) * (1.0 + mod_ref[:, 1:2, :]) + mod_ref[:, 0:1, :]
    z = jnp.dot(h.astype(BF16).reshape(nb * tb, d), w_ref[...], preferred_element_type=F32)
    z = z.reshape(nb, tb, D_IN)
    k = z[:, :, D_ATTN:2 * D_ATTN]
    v = z[:, :, 2 * D_ATTN:3 * D_ATTN]
    q_ref[...] = (z[:, :, :D_ATTN] * (SB_HEAD_DIM ** -0.5)).astype(BF16)
    k_ref[...] = k
    v_ref[...] = v
    p_ref[...] = z[:, :, 3 * D_ATTN:]
    kb_ref[...] = k.astype(BF16)
    vb_ref[...] = v.astype(BF16)


def _token_blocking(n_batch, length):
    tb = min(length, TOKEN_BLOCK)
    nb = TOKEN_BLOCK // tb
    assert length % tb == 0 and n_batch % nb == 0 and tb % 8 == 0
    return nb, tb


def _inproj(x, mod, g_pre1, w_in_bf):
    n, length, d = x.shape
    nb, tb = _token_blocking(n, length)
    tok = lambda w: pl.BlockSpec((nb, tb, w), lambda b, i: (b, i, 0))
    out = lambda dt: jax.ShapeDtypeStruct((n, length, D_ATTN), dt)
    return pl.pallas_call(
        _inproj_kernel,
        grid=(n // nb, length // tb),
        in_specs=[tok(d),
                  pl.BlockSpec((nb, 6, d), lambda b, i: (b, 0, 0)),
                  pl.BlockSpec((1, d), lambda b, i: (0, 0)),
                  pl.BlockSpec((d, D_IN), lambda b, i: (0, 0))],
        out_specs=[tok(D_ATTN)] * 6,
        out_shape=[out(BF16), out(F32), out(F32), out(F32), out(BF16), out(BF16)],
        compiler_params=_params("arbitrary", "arbitrary"),
        name="inproj",
    )(x, mod, g_pre1, w_in_bf)


def _suffix_matrix(tk):
    shape = (2 * tk, tk + LANES)
    r = lax.broadcasted_iota(jnp.int32, shape, 0)
    r = jnp.where(r >= tk, r - tk, r)
    c = lax.broadcasted_iota(jnp.int32, shape, 1)
    return jnp.where((r > c) | (c >= tk), 1.0, 0.0).astype(BF16)


def _sb_step(z, vb, carry, suffix_mat, mask):
    tk = z.shape[1]
    sp = jnp.maximum(z, 0.0) + jnp.log1p(jnp.exp(-jnp.abs(z)))
    l = -sp if mask is None else jnp.where(mask, -sp, 0.0)
    l_hi = l.astype(BF16)
    l_lo = (l - l_hi.astype(F32)).astype(BF16)
    st = jnp.dot(jnp.concatenate([l_hi, l_lo], axis=1), suffix_mat, preferred_element_type=F32)
    e = z - sp + st[:, :tk]
    if carry is not None:
        e = e + jnp.concatenate([carry] * (tk // LANES), axis=1)
    w = jnp.exp(e)
    if mask is not None:
        w = jnp.where(mask, w, 0.0)
    pv = jnp.dot(w.astype(BF16), vb, preferred_element_type=F32)
    total = st[:, tk:]
    return (total if carry is None else carry + total), pv


def _attn_kernel(*refs, n_hist_windows):
    if n_hist_windows:
        q_ref, k_ref, v_ref, hk_ref, hv_ref, o_ref = refs
    else:
        q_ref, k_ref, v_ref, o_ref = refs
    tq = q_ref.shape[1]
    n_tiles = q_ref.shape[2] // LANES
    i = pl.program_id(2)
    nt = (((1,), (1,)), ((), ()))
    tile = lambda t: slice(t * LANES, (t + 1) * LANES)
    lane = lax.broadcasted_iota(jnp.int32, (tq, LANES), 1)
    in_head = [(lane >= h * SB_HEAD_DIM) & (lane < (h + 1) * SB_HEAD_DIM) for h in range(HEADS_PER_LANE_TILE)]
    qh = [[jnp.where(m, q_ref[0, :, tile(t)], jnp.zeros((tq, LANES), BF16)) for m in in_head]
          for t in range(n_tiles)]

    def step(read, carries, accs, mat, mask):
        new_carries, new_accs = [], []
        for t in range(n_tiles):
            kb, vb = read(t)
            pvs = []
            for h in range(HEADS_PER_LANE_TILE):
                z = lax.dot_general(qh[t][h], kb, nt, preferred_element_type=F32)
                c, pv = _sb_step(z, vb, carries[t * HEADS_PER_LANE_TILE + h], mat, mask)
                new_carries.append(c)
                pvs.append(pv)
            out = pvs[-1]
            for h in range(HEADS_PER_LANE_TILE - 1):
                out = jnp.where(in_head[h], pvs[h], out)
            new_accs.append(out if accs[t] is None else accs[t] + out)
        return tuple(new_carries), tuple(new_accs), jnp.max(functools.reduce(jnp.maximum, new_carries))

    def sweep(state, k_src, v_src, mat, partial_last):
        per = SB_KEY_WINDOW // SB_KEY_BLOCK
        col = lax.broadcasted_iota(jnp.int32, (tq, SB_KEY_WINDOW), 1)

        def body(s):
            r = s[0]
            first = jnp.maximum(r - per, 0)
            start = pl.multiple_of(first * SB_KEY_BLOCK, SB_KEY_BLOCK)
            rows = pl.ds(start, SB_KEY_WINDOW)
            mask = col < (r - first) * SB_KEY_BLOCK if partial_last else None
            read = lambda t: (k_src[0, rows, tile(t)].astype(BF16), v_src[0, rows, tile(t)].astype(BF16))
            return (r - per,) + step(read, s[1], s[2], mat, mask)
        return lax.while_loop(lambda s: (s[0] > 0) & (s[3] > EXP_UNDERFLOW), body, state)

    causal = (lax.broadcasted_iota(jnp.int32, (tq, tq), 1) < lax.broadcasted_iota(jnp.int32, (tq, tq), 0))
    rows = pl.ds(pl.multiple_of(i * tq, tq), tq)
    state = (i,) + step(lambda t: (k_ref[0, rows, tile(t)], v_ref[0, rows, tile(t)]),
                        (None,) * (n_tiles * HEADS_PER_LANE_TILE), (None,) * n_tiles, _suffix_matrix(tq), causal)
    win_mat = _suffix_matrix(SB_KEY_WINDOW)
    if tq == SB_KEY_BLOCK:
        assert k_ref.shape[1] >= SB_KEY_WINDOW
        state = sweep(state, k_ref, v_ref, win_mat, True)
    else:
        assert q_ref.shape[1] == k_ref.shape[1]
    if n_hist_windows:
        per = SB_KEY_WINDOW // SB_KEY_BLOCK
        state = sweep((jnp.int32(n_hist_windows * per),) + state[1:], hk_ref, hv_ref, win_mat, False)
    o_ref[0] = jnp.concatenate(state[2], axis=1)


def _attn(q_bf, k_bf, v_bf, hist_k=None, hist_v=None):
    n, length, _ = q_bf.shape
    tq = min(length, SB_KEY_BLOCK)
    width = SB_LANE_TILES * LANES
    qspec = pl.BlockSpec((1, tq, width), lambda b, t, i: (b, i, t))
    kspec = pl.BlockSpec((1, length, width), lambda b, t, i: (b, 0, t))
    in_specs = [qspec, kspec, kspec]
    args = [q_bf, k_bf, v_bf]
    n_hist_windows = 0
    if hist_k is not None:
        past = hist_k.shape[1]
        assert past % SB_KEY_WINDOW == 0
        n_hist_windows = past // SB_KEY_WINDOW
        hspec = pl.BlockSpec((1, past, width), lambda b, t, i: (b, 0, t))
        in_specs += [hspec, hspec]
        args += [hist_k, hist_v]
    return pl.pallas_call(
        functools.partial(_attn_kernel, n_hist_windows=n_hist_windows),
        grid=(n, D_ATTN // width, length // tq),
        in_specs=in_specs,
        out_specs=qspec,
        out_shape=jax.ShapeDtypeStruct((n, length, D_ATTN), F32),
        compiler_params=_params("arbitrary", "arbitrary", "arbitrary"),
        name="attn",
    )(*args)


def _post_kernel(oa_ref, p_ref, pprev_ref, hist_ref, x_ref, mod_ref, goa_ref, gob_ref, wpool_ref,
                 pscale_ref, wout_ref, gpost1_ref, gpre2_ref, wq_ref,
                 x1_ref, h2_ref, qp_ref, xp_ref, *, offset):
    nb, tb, d = x_ref.shape
    i = pl.program_id(1)
    p = p_ref[...]
    first = jnp.broadcast_to(i == 0, (nb, POOL_TAIL, D_POOL))
    xp_ref[:, :POOL_TAIL, :] = jnp.where(first, hist_ref[...], pprev_ref[:, tb - POOL_TAIL:, :])
    xp_ref[:, POOL_TAIL:, :] = p
    pos = offset + i * tb + lax.broadcasted_iota(jnp.int32, (1, tb, 1), 1)
    mixed = []
    for g, w in enumerate(POOL_WINDOWS):
        lanes = slice(g * POOL_GROUP, (g + 1) * POOL_GROUP)
        s = p[:, :, lanes]
        for back in range(1, w):
            s = s + xp_ref[:, pl.ds(POOL_TAIL - back, tb), lanes]
        cnt = jnp.minimum(pos + 1, w).astype(F32)
        pooled = (s / cnt - p[:, :, lanes]).reshape(nb * tb, POOL_GROUP)
        mixed.append(jnp.dot(pooled.astype(BF16), wpool_ref[g], preferred_element_type=F32))
    o_b = jnp.concatenate(mixed, axis=-1) * pscale_ref[...]
    o_a = oa_ref[...].reshape(nb * tb, D_ATTN)
    cat = jnp.concatenate([_rms(o_a, goa_ref[...]), _rms(o_b, gob_ref[...])], axis=-1)
    o = jnp.dot(cat.astype(BF16), wout_ref[...], preferred_element_type=F32)
    x1 = x_ref[...] + mod_ref[:, 2:3, :] * _rms(o, gpost1_ref[...]).reshape(nb, tb, d)
    x1_ref[...] = x1
    h2 = _rms(x1, gpre2_ref[...]) * (1.0 + mod_ref[:, 4:5, :]) + mod_ref[:, 3:4, :]
    h2_ref[...] = h2
    qp = jnp.dot(h2.astype(BF16).reshape(nb * tb, d), wq_ref[...], preferred_element_type=F32)
    qp_ref[...] = qp.reshape(nb, tb, qp_ref.shape[2])


def _post(o_a, p, hist, x, mod, g_out_a, g_out_b, w_pool_bf, pool_scale, w_out_bf, g_post1, g_pre2,
          w_query_bf, offset):
    n, length, d = x.shape
    nb, tb = _token_blocking(n, length)
    assert tb >= POOL_TAIL
    dq = w_query_bf.shape[1]
    tok = lambda w: pl.BlockSpec((nb, tb, w), lambda b, i: (b, i, 0))
    full = lambda a: pl.BlockSpec(a.shape, lambda b, i: (0,) * a.ndim)
    return pl.pallas_call(
        functools.partial(_post_kernel, offset=offset),
        grid=(n // nb, length // tb),
        in_specs=[tok(D_ATTN), tok(D_POOL),
                  pl.BlockSpec((nb, tb, D_POOL), lambda b, i: (b, jnp.maximum(i - 1, 0), 0)),
                  pl.BlockSpec((nb, POOL_TAIL, D_POOL), lambda b, i: (b, 0, 0)),
                  tok(d),
                  pl.BlockSpec((nb, 6, d), lambda b, i: (b, 0, 0)),
                  full(g_out_a), full(g_out_b), full(w_pool_bf), full(pool_scale), full(w_out_bf),
                  full(g_post1), full(g_pre2), full(w_query_bf)],
        out_specs=[tok(d), tok(d), tok(dq)],
        out_shape=[jax.ShapeDtypeStruct((n, length, d), F32),
                   jax.ShapeDtypeStruct((n, length, d), F32),
                   jax.ShapeDtypeStruct((n, length, dq), F32)],
        scratch_shapes=[pltpu.VMEM((nb, tb + POOL_TAIL, D_POOL), F32)],
        compiler_params=_params("arbitrary", "arbitrary"),
        name="post",
    )(o_a, p, p, hist, x, mod, g_out_a, g_out_b, w_pool_bf, pool_scale, w_out_bf, g_post1, g_pre2,
      w_query_bf)


def _topk_rows(s, k, order):
    vals, picks = [], []
    for _ in range(k):
        m = jnp.max(s, axis=0, keepdims=True)
        pick = jnp.min(jnp.where(s == m, order, jnp.inf), axis=0, keepdims=True)
        vals.append(m)
        picks.append(pick)
        s = jnp.where(order == pick, -jnp.inf, s)
    return jnp.concatenate(vals, axis=0), jnp.concatenate(picks, axis=0).astype(jnp.int32)


def _select_rows(table, sel):
    out = jnp.zeros_like(table)
    for r in range(table.shape[0]):
        out = jnp.where(sel == r, table[r:r + 1, :], out)
    return out


def _pair_candidates(v1, v2):
    tb = v1.shape[1]
    sub = lambda n: lax.broadcasted_iota(jnp.int32, (n, tb), 0)
    vals, flat = [], []
    for a, nb in ((0, 16), (1, 8), (2, 8), (3, 8)):
        vals.append(v1[a:a + 1, :] + v2[:nb, :])
        flat.append(a * PEER_TOPK + sub(nb))
    vals.append(v1[8:16, :] + v2[0:1, :])
    flat.append((8 + sub(8)) * PEER_TOPK)
    for b in range(3):
        vals.append(jnp.where(sub(8) >= 4, v1[0:8, :] + v2[b:b + 1, :], -jnp.inf))
        flat.append(sub(8) * PEER_TOPK + b)
    return jnp.concatenate(vals, axis=0), jnp.concatenate(flat, axis=0).astype(F32)


def _route_kernel(qp_ref, keys_ref, idx_ref, gate_ref):
    nt = (((1,), (1,)), ((), ()))
    tb = qp_ref.shape[0]
    key_order = lax.broadcasted_iota(jnp.int32, (N_KEYS, tb), 0).astype(F32)
    idx_rows, gate_rows = [], []
    for h in range(PEER_HEADS):
        base = h * 2 * KEY_HALF
        q1 = qp_ref[:, base:base + KEY_HALF].astype(BF16)
        q2 = qp_ref[:, base + KEY_HALF:base + 2 * KEY_HALF].astype(BF16)
        s1 = lax.dot_general(keys_ref[0, h], q1, nt, preferred_element_type=F32)
        s2 = lax.dot_general(keys_ref[1, h], q2, nt, preferred_element_type=F32)
        v1, i1 = _topk_rows(s1, PEER_TOPK, key_order)
        v2, i2 = _topk_rows(s2, PEER_TOPK, key_order)
        cand, flat = _pair_candidates(v1, v2)
        best, sel = _topk_rows(cand, PEER_TOPK, flat)
        a_sel = lax.shift_right_logical(sel, 4)
        b_sel = sel & (PEER_TOPK - 1)
        idx_rows.append(_select_rows(i1, a_sel) * N_KEYS + _select_rows(i2, b_sel))
        e = jnp.exp(best - best[0:1, :])
        gate_rows.append(e / jnp.sum(e, axis=0, keepdims=True))
    idx_ref[...] = jnp.concatenate(idx_rows, axis=0).T
    gate_ref[...] = jnp.concatenate(gate_rows, axis=0).T


def _route(qp, sub_keys_bf):
    t, dq = qp.shape
    tb = min(t, TOKEN_BLOCK)
    assert t % tb == 0 and PEER_TOPK == 16
    return pl.pallas_call(
        _route_kernel,
        grid=(t // tb,),
        in_specs=[pl.BlockSpec((tb, dq), lambda i: (i, 0)),
                  pl.BlockSpec(sub_keys_bf.shape, lambda i: (0, 0, 0, 0))],
        out_specs=[pl.BlockSpec((tb, PEER_SEL), lambda i: (i, 0))] * 2,
        out_shape=[jax.ShapeDtypeStruct((t, PEER_SEL), jnp.int32),
                   jax.ShapeDtypeStruct((t, PEER_SEL), F32)],
        compiler_params=_params("arbitrary"),
        name="route",
    )(qp, sub_keys_bf)


def _peer_kernel(idx_ref, gate_ref, h_ref, uv_ref, o_ref, buf_ref, sem_ref):
    tg, d = h_ref.shape
    eye = (lax.broadcasted_iota(jnp.int32, (PEER_SEL, PEER_SEL), 0)
           == lax.broadcasted_iota(jnp.int32, (PEER_SEL, PEER_SEL), 1))

    def issue(t, slot):
        for j in range(PEER_SEL):
            pltpu.make_async_copy(uv_ref.at[idx_ref[t, j]], buf_ref.at[slot, pl.ds(j, 1)],
                                  sem_ref.at[slot]).start(priority=j % 2)

    def wait(slot):
        pltpu.make_async_copy(uv_ref.at[pl.ds(0, PEER_SEL), 0], buf_ref.at[slot], sem_ref.at[slot]).wait()

    def activations(t, slot):
        hrow = h_ref[pl.ds(t, 1), :]
        s_col = jnp.sum(buf_ref[slot, :, :d] * hrow, axis=1, keepdims=True)
        s_row = jnp.sum(jnp.where(eye, s_col, 0.0), axis=0, keepdims=True)
        return gate_ref[pl.ds(t, 1), :] * _gelu(s_row)

    def combine(t, slot, a_row):
        a_col = jnp.sum(jnp.where(eye, a_row, 0.0), axis=1, keepdims=True)
        o_ref[pl.ds(t, 1), :] = jnp.sum(a_col * buf_ref[slot, :, d:], axis=0, keepdims=True)

    for s in range(PEER_SLOTS):
        issue(s, s)
    wait(0)
    first = activations(0, 0)

    def group(g, a_row):
        for s in range(PEER_SLOTS):
            t = g * PEER_SLOTS + s
            nxt = (s + 1) % PEER_SLOTS
            wait(nxt)
            a_next = activations(t + 1, nxt)
            combine(t, s, a_row)
            issue(t + PEER_SLOTS, s)
            a_row = a_next
        return a_row

    n_groups = tg // PEER_SLOTS
    a_row = lax.fori_loop(0, n_groups - 1, group, first)
    for s in range(PEER_SLOTS):
        t = (n_groups - 1) * PEER_SLOTS + s
        if s + 1 < PEER_SLOTS:
            wait(s + 1)
            a_next = activations(t + 1, s + 1)
        combine(t, s, a_row)
        a_row = a_next


def _peer(idx, gate, h2, uv):
    t, d = h2.shape
    tg = PEER_TOKEN_BLOCK
    assert t % tg == 0 and tg % PEER_SLOTS == 0 and uv.shape[1:] == (1, 2 * d)
    return pl.pallas_call(
        _peer_kernel,
        grid=(t // tg,),
        in_specs=[pl.BlockSpec((tg, PEER_SEL), lambda i: (i, 0), memory_space=pltpu.SMEM),
                  pl.BlockSpec((tg, PEER_SEL), lambda i: (i, 0)),
                  pl.BlockSpec((tg, d), lambda i: (i, 0)),
                  pl.BlockSpec(memory_space=pl.ANY)],
        out_specs=pl.BlockSpec((tg, d), lambda i: (i, 0)),
        out_shape=jax.ShapeDtypeStruct((t, d), F32),
        scratch_shapes=[pltpu.VMEM((PEER_SLOTS, PEER_SEL, 2 * d), F32),
                        pltpu.SemaphoreType.DMA((PEER_SLOTS,))],
        compiler_params=_params("arbitrary"),
        name="peer",
    )(idx, gate, h2, uv)


def _final_kernel(x1_ref, peer_ref, mod_ref, g_ref, y_ref):
    y_ref[...] = x1_ref[...] + mod_ref[:, 5:6, :] * _rms(peer_ref[...], g_ref[...])


def _final(x1, peer, mod, g_post2):
    n, length, d = x1.shape
    nb, tb = _token_blocking(n, length)
    tok = pl.BlockSpec((nb, tb, d), lambda b, i: (b, i, 0))
    return pl.pallas_call(
        _final_kernel,
        grid=(n // nb, length // tb),
        in_specs=[tok, tok, pl.BlockSpec((nb, 6, d), lambda b, i: (b, 0, 0)),
                  pl.BlockSpec((1, d), lambda b, i: (0, 0))],
        out_specs=tok,
        out_shape=jax.ShapeDtypeStruct((n, length, d), F32),
        compiler_params=_params("arbitrary", "arbitrary"),
        name="final",
    )(x1, peer, mod, g_post2)


def _layer(x, mod, hist_k, hist_v, hist_p, wts):
    (g_pre1, g_post1, g_pre2, g_post2, w_in_bf, g_out_a, g_out_b, w_pool_bf, pool_scale, w_out_bf,
     w_query_bf, sub_keys_bf, uv) = wts
    n, length, d = x.shape
    q_bf, k, v, p, k_bf, v_bf = _inproj(x, mod, g_pre1, w_in_bf)
    offset = 0 if hist_k is None else hist_k.shape[1]
    o_a = _attn(q_bf, k_bf, v_bf, hist_k, hist_v)
    x1, h2, qp = _post(o_a, p, hist_p, x, mod, g_out_a, g_out_b, w_pool_bf, pool_scale, w_out_bf,
                       g_post1, g_pre2, w_query_bf, offset)
    idx, gate = _route(qp.reshape(n * length, -1), sub_keys_bf)
    peer = _peer(idx, gate, h2.reshape(n * length, d), uv)
    y = _final(x1, peer.reshape(n, length, d), mod, g_post2)
    heads = (n, length, SB_HEADS, SB_HEAD_DIM)
    p_state = jnp.concatenate([hist_p[:, 1:], p], axis=1)[:, -POOL_HIST:]
    return y, k.reshape(heads), v.reshape(heads), p_state


def kernel(x_prompt, x_sample, cache_k, cache_v, state_pool, c_prompt, c_sample, w_ada, b_ada, g_pre1, g_post1, g_pre2, g_post2, w_in, g_out_a, g_out_b, w_pool, pool_scale, w_out, w_query, sub_keys, u_experts, v_experts):
    depth = w_ada.shape[0]
    assert depth == 1
    bp, bs = x_prompt.shape[0], x_sample.shape[0]
    row = lambda a: a.reshape(1, -1)
    c_all = jnp.concatenate([c_prompt, c_sample], axis=0)
    pad = (-c_all.shape[0]) % 8
    mod = _mod(jnp.pad(c_all, ((0, pad), (0, 0))), w_ada[0], b_ada[0]).reshape(-1, 6, D_MODEL)
    wts = (row(g_pre1[0]), row(g_post1[0]), row(g_pre2[0]), row(g_post2[0]), w_in[0].astype(BF16),
           row(g_out_a[0]), row(g_out_b[0]), w_pool[0].astype(BF16), row(pool_scale[0]),
           w_out[0].astype(BF16), w_query[0].astype(BF16), sub_keys[0].astype(BF16),
           jnp.concatenate([u_experts[0][:, None, :], v_experts[0][:, None, :]], axis=2))
    past = cache_k.shape[2]
    hist_p = jnp.pad(state_pool[0], ((0, 0), (POOL_TAIL - POOL_HIST, 0), (0, 0)))
    y_s, k_s, v_s, p_s = _layer(x_sample, mod[bp:bp + bs], cache_k[0].reshape(bs, past, D_ATTN),
                                cache_v[0].reshape(bs, past, D_ATTN), hist_p, wts)
    y_p, k_p, v_p, p_p = _layer(x_prompt, mod[:bp], None, None,
                                jnp.zeros((bp, POOL_TAIL, D_POOL), F32), wts)
    return (y_p, y_s, k_p[None], v_p[None], p_p[None], k_s[None], v_s[None], p_s[None])
```

```python
import functools

import jax
import jax.numpy as jnp
from jax import lax
from jax.experimental import pallas as pl
from jax.experimental.pallas import tpu as pltpu

F32 = jnp.float32
BF16 = jnp.bfloat16

D_MODEL = 1024
D_ATTN = D_MODEL // 2
SB_HEADS = 8
SB_HEAD_DIM = D_ATTN // SB_HEADS
D_POOL = D_MODEL - D_ATTN
POOL_WINDOWS = (2, 4, 8, 16)
POOL_GROUP = D_POOL // len(POOL_WINDOWS)
POOL_HIST = max(POOL_WINDOWS) - 1
POOL_TAIL = POOL_HIST + 1
D_IN = 3 * D_ATTN + D_POOL
N_KEYS = 128
PEER_HEADS = 8
PEER_TOPK = 16
KEY_HALF = 128
PEER_SEL = PEER_HEADS * PEER_TOPK
EPS = 1e-6

LANES = 128
HEADS_PER_LANE_TILE = LANES // SB_HEAD_DIM
SB_KEY_BLOCK = 128
SB_KEY_WINDOW = 256
SB_LANE_TILES = 4
EXP_UNDERFLOW = -104.0

TOKEN_BLOCK = 256
PEER_TOKEN_BLOCK = 256
PEER_SLOTS = 8
VMEM_LIMIT = 48 * 1024 * 1024


def _params(*sem):
    return pltpu.CompilerParams(dimension_semantics=sem, vmem_limit_bytes=VMEM_LIMIT)


def _rms(x, g):
    ms = jnp.mean(x * x, axis=-1, keepdims=True)
    return x * lax.rsqrt(ms + EPS) * g


def _gelu(x):
    return 0.5 * x * (1.0 + lax.erf(x * (2.0 ** -0.5)))


def _mod_kernel(c_ref, w_ref, b_ref, o_ref):
    s = jax.nn.silu(c_ref[...])
    o_ref[...] = jnp.dot(s, w_ref[...], precision=lax.Precision.HIGHEST,
                         preferred_element_type=F32) + b_ref[...]


def _mod(c, w_ada, b_ada):
    n, d = c.shape
    dout = w_ada.shape[1]
    return pl.pallas_call(
        _mod_kernel,
        grid=(dout // d,),
        in_specs=[pl.BlockSpec((n, d), lambda j: (0, 0)),
                  pl.BlockSpec((d, d), lambda j: (0, j)),
                  pl.BlockSpec((1, d), lambda j: (0, j))],
        out_specs=pl.BlockSpec((n, d), lambda j: (0, j)),
        out_shape=jax.ShapeDtypeStruct((n, dout), F32),
        compiler_params=_params("arbitrary"),
        name="mod",
    )(c, w_ada, b_ada.reshape(1, dout))


def _inproj_kernel(x_ref, mod_ref, g_ref, w_ref, q_ref, k_ref, v_ref, p_ref, kb_ref, vb_ref):
    nb, tb, d = x_ref.shape
    h = _rms(x_ref[...], g_ref[...]) * (1.0 + mod_ref[:, 1:2, :]) + mod_ref[:, 0:1, :]
    z = jnp.dot(h.astype(BF16).reshape(nb * tb, d), w_ref[...], preferred_element_type=F32)
    z = z.reshape(nb, tb, D_IN)
    k = z[:, :, D_ATTN:2 * D_ATTN]
    v = z[:, :, 2 * D_ATTN:3 * D_ATTN]
    q_ref[...] = (z[:, :, :D_ATTN] * (SB_HEAD_DIM ** -0.5)).astype(BF16)
    k_ref[...] = k
    v_ref[...] = v
    p_ref[...] = z[:, :, 3 * D_ATTN:]
    kb_ref[...] = k.astype(BF16)
    vb_ref[...] = v.astype(BF16)


def _token_blocking(n_batch, length):
    tb = min(length, TOKEN_BLOCK)
    nb = TOKEN_BLOCK // tb
    assert length % tb == 0 and n_batch % nb == 0 and tb % 8 == 0
    return nb, tb


def _inproj(x, mod, g_pre1, w_in_bf):
    n, length, d = x.shape
    nb, tb = _token_blocking(n, length)
    tok = lambda w: pl.BlockSpec((nb, tb, w), lambda b, i: (b, i, 0))
    out = lambda dt: jax.ShapeDtypeStruct((n, length, D_ATTN), dt)
    return pl.pallas_call(
        _inproj_kernel,
        grid=(n // nb, length // tb),
        in_specs=[tok(d),
                  pl.BlockSpec((nb, 6, d), lambda b, i: (b, 0, 0)),
                  pl.BlockSpec((1, d), lambda b, i: (0, 0)),
                  pl.BlockSpec((d, D_IN), lambda b, i: (0, 0))],
        out_specs=[tok(D_ATTN)] * 6,
        out_shape=[out(BF16), out(F32), out(F32), out(F32), out(BF16), out(BF16)],
        compiler_params=_params("arbitrary", "arbitrary"),
        name="inproj",
    )(x, mod, g_pre1, w_in_bf)


def _suffix_matrix(tk):
    shape = (2 * tk, tk + LANES)
    r = lax.broadcasted_iota(jnp.int32, shape, 0)
    r = jnp.where(r >= tk, r - tk, r)
    c = lax.broadcasted_iota(jnp.int32, shape, 1)
    return jnp.where((r > c) | (c >= tk), 1.0, 0.0).astype(BF16)


def _sb_step(z, vb, carry, suffix_mat, mask):
    tk = z.shape[1]
    sp = jnp.maximum(z, 0.0) + jnp.log1p(jnp.exp(-jnp.abs(z)))
    l = -sp if mask is None else jnp.where(mask, -sp, 0.0)
    l_hi = l.astype(BF16)
    l_lo = (l - l_hi.astype(F32)).astype(BF16)
    st = jnp.dot(jnp.concatenate([l_hi, l_lo], axis=1), suffix_mat, preferred_element_type=F32)
    e = z - sp + st[:, :tk]
    if carry is not None:
        e = e + jnp.concatenate([carry] * (tk // LANES), axis=1)
    w = jnp.exp(e)
    if mask is not None:
        w = jnp.where(mask, w, 0.0)
    pv = jnp.dot(w.astype(BF16), vb, preferred_element_type=F32)
    total = st[:, tk:]
    return (total if carry is None else carry + total), pv


def _attn_kernel(*refs, n_hist_windows):
    if n_hist_windows:
        q_ref, k_ref, v_ref, hk_ref, hv_ref, o_ref = refs
    else:
        q_ref, k_ref, v_ref, o_ref = refs
    tq = q_ref.shape[1]
    n_tiles = q_ref.shape[2] // LANES
    i = pl.program_id(2)
    nt = (((1,), (1,)), ((), ()))
    tile = lambda t: slice(t * LANES, (t + 1) * LANES)
    lane = lax.broadcasted_iota(jnp.int32, (tq, LANES), 1)
    in_head = [(lane >= h * SB_HEAD_DIM) & (lane < (h + 1) * SB_HEAD_DIM) for h in range(HEADS_PER_LANE_TILE)]
    qh = [[jnp.where(m, q_ref[0, :, tile(t)], jnp.zeros((tq, LANES), BF16)) for m in in_head]
          for t in range(n_tiles)]

    def step(read, carries, accs, mat, mask):
        new_carries, new_accs = [], []
        for t in range(n_tiles):
            kb, vb = read(t)
            pvs = []
            for h in range(HEADS_PER_LANE_TILE):
                z = lax.dot_general(qh[t][h], kb, nt, preferred_element_type=F32)
                c, pv = _sb_step(z, vb, carries[t * HEADS_PER_LANE_TILE + h], mat, mask)
                new_carries.append(c)
                pvs.append(pv)
            out = pvs[-1]
            for h in range(HEADS_PER_LANE_TILE - 1):
                out = jnp.where(in_head[h], pvs[h], out)
            new_accs.append(out if accs[t] is None else accs[t] + out)
        return tuple(new_carries), tuple(new_accs), jnp.max(functools.reduce(jnp.maximum, new_carries))

    def sweep(state, k_src, v_src, mat, partial_last):
        per = SB_KEY_WINDOW // SB_KEY_BLOCK
        col = lax.broadcasted_iota(jnp.int32, (tq, SB_KEY_WINDOW), 1)

        def body(s):
            r = s[0]
            first = jnp.maximum(r - per, 0)
            start = pl.multiple_of(first * SB_KEY_BLOCK, SB_KEY_BLOCK)
            rows = pl.ds(start, SB_KEY_WINDOW)
            mask = col < (r - first) * SB_KEY_BLOCK if partial_last else None
            read = lambda t: (k_src[0, rows, tile(t)].astype(BF16), v_src[0, rows, tile(t)].astype(BF16))
            return (r - per,) + step(read, s[1], s[2], mat, mask)
        return lax.while_loop(lambda s: (s[0] > 0) & (s[3] > EXP_UNDERFLOW), body, state)

    causal = (lax.broadcasted_iota(jnp.int32, (tq, tq), 1) < lax.broadcasted_iota(jnp.int32, (tq, tq), 0))
    rows = pl.ds(pl.multiple_of(i * tq, tq), tq)
    state = (i,) + step(lambda t: (k_ref[0, rows, tile(t)], v_ref[0, rows, tile(t)]),
                        (None,) * (n_tiles * HEADS_PER_LANE_TILE), (None,) * n_tiles, _suffix_matrix(tq), causal)
    win_mat = _suffix_matrix(SB_KEY_WINDOW)
    if tq == SB_KEY_BLOCK:
        assert k_ref.shape[1] >= SB_KEY_WINDOW
        state = sweep(state, k_ref, v_ref, win_mat, True)
    else:
        assert q_ref.shape[1] == k_ref.shape[1]
    if n_hist_windows:
        per = SB_KEY_WINDOW // SB_KEY_BLOCK
        state = sweep((jnp.int32(n_hist_windows * per),) + state[1:], hk_ref, hv_ref, win_mat, False)
    o_ref[0] = jnp.concatenate(state[2], axis=1)


def _attn(q_bf, k_bf, v_bf, hist_k=None, hist_v=None):
    n, length, _ = q_bf.shape
    tq = min(length, SB_KEY_BLOCK)
    width = SB_LANE_TILES * LANES
    qspec = pl.BlockSpec((1, tq, width), lambda b, t, i: (b, i, t))
    kspec = pl.BlockSpec((1, length, width), lambda b, t, i: (b, 0, t))
    in_specs = [qspec, kspec, kspec]
    args = [q_bf, k_bf, v_bf]
    n_hist_windows = 0
    if hist_k is not None:
        past = hist_k.shape[1]
        assert past % SB_KEY_WINDOW == 0
        n_hist_windows = past // SB_KEY_WINDOW
        hspec = pl.BlockSpec((1, past, width), lambda b, t, i: (b, 0, t))
        in_specs += [hspec, hspec]
        args += [hist_k, hist_v]
    return pl.pallas_call(
        functools.partial(_attn_kernel, n_hist_windows=n_hist_windows),
        grid=(n, D_ATTN // width, length // tq),
        in_specs=in_specs,
        out_specs=qspec,
        out_shape=jax.ShapeDtypeStruct((n, length, D_ATTN), F32),
        compiler_params=_params("arbitrary", "arbitrary", "arbitrary"),
        name="attn",
    )(*args)


def _post_kernel(oa_ref, p_ref, pprev_ref, hist_ref, x_ref, mod_ref, goa_ref, gob_ref, wpool_ref,
                 pscale_ref, wout_ref, gpost1_ref, gpre2_ref, wq_ref,
                 x1_ref, h2_ref, qp_ref, xp_ref, *, offset):
    nb, tb, d = x_ref.shape
    i = pl.program_id(1)
    p = p_ref[...]
    first = jnp.broadcast_to(i == 0, (nb, POOL_TAIL, D_POOL))
    xp_ref[:, :POOL_TAIL, :] = jnp.where(first, hist_ref[...], pprev_ref[:, tb - POOL_TAIL:, :])
    xp_ref[:, POOL_TAIL:, :] = p
    pos = offset + i * tb + lax.broadcasted_iota(jnp.int32, (1, tb, 1), 1)
    mixed = []
    for g, w in enumerate(POOL_WINDOWS):
        lanes = slice(g * POOL_GROUP, (g + 1) * POOL_GROUP)
        s = p[:, :, lanes]
        for back in range(1, w):
            s = s + xp_ref[:, pl.ds(POOL_TAIL - back, tb), lanes]
        cnt = jnp.minimum(pos + 1, w).astype(F32)
        pooled = (s / cnt - p[:, :, lanes]).reshape(nb * tb, POOL_GROUP)
        mixed.append(jnp.dot(pooled.astype(BF16), wpool_ref[g], preferred_element_type=F32))
    o_b = jnp.concatenate(mixed, axis=-1) * pscale_ref[...]
    o_a = oa_ref[...].reshape(nb * tb, D_ATTN)
    cat = jnp.concatenate([_rms(o_a, goa_ref[...]), _rms(o_b, gob_ref[...])], axis=-1)
    o = jnp.dot(cat.astype(BF16), wout_ref[...], preferred_element_type=F32)
    x1 = x_ref[...] + mod_ref[:, 2:3, :] * _rms(o, gpost1_ref[...]).reshape(nb, tb, d)
    x1_ref[...] = x1
    h2 = _rms(x1, gpre2_ref[...]) * (1.0 + mod_ref[:, 4:5, :]) + mod_ref[:, 3:4, :]
    h2_ref[...] = h2
    qp = jnp.dot(h2.astype(BF16).reshape(nb * tb, d), wq_ref[...], preferred_element_type=F32)
    qp_ref[...] = qp.reshape(nb, tb, qp_ref.shape[2])


def _post(o_a, p, hist, x, mod, g_out_a, g_out_b, w_pool_bf, pool_scale, w_out_bf, g_post1, g_pre2,
          w_query_bf, offset):
    n, length, d = x.shape
    nb, tb = _token_blocking(n, length)
    assert tb >= POOL_TAIL
    dq = w_query_bf.shape[1]
    tok = lambda w: pl.BlockSpec((nb, tb, w), lambda b, i: (b, i, 0))
    full = lambda a: pl.BlockSpec(a.shape, lambda b, i: (0,) * a.ndim)
    return pl.pallas_call(
        functools.partial(_post_kernel, offset=offset),
        grid=(n // nb, length // tb),
        in_specs=[tok(D_ATTN), tok(D_POOL),
                  pl.BlockSpec((nb, tb, D_POOL), lambda b, i: (b, jnp.maximum(i - 1, 0), 0)),
                  pl.BlockSpec((nb, POOL_TAIL, D_POOL), lambda b, i: (b, 0, 0)),
                  tok(d),
                  pl.BlockSpec((nb, 6, d), lambda b, i: (b, 0, 0)),
                  full(g_out_a), full(g_out_b), full(w_pool_bf), full(pool_scale), full(w_out_bf),
                  full(g_post1), full(g_pre2), full(w_query_bf)],
        out_specs=[tok(d), tok(d), tok(dq)],
        out_shape=[jax.ShapeDtypeStruct((n, length, d), F32),
                   jax.ShapeDtypeStruct((n, length, d), F32),
                   jax.ShapeDtypeStruct((n, length, dq), F32)],
        scratch_shapes=[pltpu.VMEM((nb, tb + POOL_TAIL, D_POOL), F32)],
        compiler_params=_params("arbitrary", "arbitrary"),
        name="post",
    )(o_a, p, p, hist, x, mod, g_out_a, g_out_b, w_pool_bf, pool_scale, w_out_bf, g_post1, g_pre2,
      w_query_bf)


def _topk_rows(s, k, order):
    vals, picks = [], []
    for _ in range(k):
        m = jnp.max(s, axis=0, keepdims=True)
        pick = jnp.min(jnp.where(s == m, order, jnp.inf), axis=0, keepdims=True)
        vals.append(m)
        picks.append(pick)
        s = jnp.where(order == pick, -jnp.inf, s)
    return jnp.concatenate(vals, axis=0), jnp.concatenate(picks, axis=0).astype(jnp.int32)


def _select_rows(table, sel):
    out = jnp.zeros_like(table)
    for r in range(table.shape[0]):
        out = jnp.where(sel == r, table[r:r + 1, :], out)
    return out


def _pair_candidates(v1, v2):
    tb = v1.shape[1]
    sub = lambda n: lax.broadcasted_iota(jnp.int32, (n, tb), 0)
    vals, flat = [], []
    for a, nb in ((0, 16), (1, 8), (2, 8), (3, 8)):
        vals.append(v1[a:a + 1, :] + v2[:nb, :])
        flat.append(a * PEER_TOPK + sub(nb))
    vals.append(v1[8:16, :] + v2[0:1, :])
    flat.append((8 + sub(8)) * PEER_TOPK)
    for b in range(3):
        vals.append(jnp.where(sub(8) >= 4, v1[0:8, :] + v2[b:b + 1, :], -jnp.inf))
        flat.append(sub(8) * PEER_TOPK + b)
    return jnp.concatenate(vals, axis=0), jnp.concatenate(flat, axis=0).astype(F32)


def _route_kernel(qp_ref, keys_ref, idx_ref, gate_ref):
    nt = (((1,), (1,)), ((), ()))
    tb = qp_ref.shape[0]
    key_order = lax.broadcasted_iota(jnp.int32, (N_KEYS, tb), 0).astype(F32)
    idx_rows, gate_rows = [], []
    for h in range(PEER_HEADS):
        base = h * 2 * KEY_HALF
        q1 = qp_ref[:, base:base + KEY_HALF].astype(BF16)
        q2 = qp_ref[:, base + KEY_HALF:base + 2 * KEY_HALF].astype(BF16)
        s1 = lax.dot_general(keys_ref[0, h], q1, nt, preferred_element_type=F32)
        s2 = lax.dot_general(keys_ref[1, h], q2, nt, preferred_element_type=F32)
        v1, i1 = _topk_rows(s1, PEER_TOPK, key_order)
        v2, i2 = _topk_rows(s2, PEER_TOPK, key_order)
        cand, flat = _pair_candidates(v1, v2)
        best, sel = _topk_rows(cand, PEER_TOPK, flat)
        a_sel = lax.shift_right_logical(sel, 4)
        b_sel = sel & (PEER_TOPK - 1)
        idx_rows.append(_select_rows(i1, a_sel) * N_KEYS + _select_rows(i2, b_sel))
        e = jnp.exp(best - best[0:1, :])
        gate_rows.append(e / jnp.sum(e, axis=0, keepdims=True))
    idx_ref[...] = jnp.concatenate(idx_rows, axis=0).T
    gate_ref[...] = jnp.concatenate(gate_rows, axis=0).T


def _route(qp, sub_keys_bf):
    t, dq = qp.shape
    tb = min(t, TOKEN_BLOCK)
    assert t % tb == 0 and PEER_TOPK == 16
    return pl.pallas_call(
        _route_kernel,
        grid=(t // tb,),
        in_specs=[pl.BlockSpec((tb, dq), lambda i: (i, 0)),
                  pl.BlockSpec(sub_keys_bf.shape, lambda i: (0, 0, 0, 0))],
        out_specs=[pl.BlockSpec((tb, PEER_SEL), lambda i: (i, 0))] * 2,
        out_shape=[jax.ShapeDtypeStruct((t, PEER_SEL), jnp.int32),
                   jax.ShapeDtypeStruct((t, PEER_SEL), F32)],
        compiler_params=_params("arbitrary"),
        name="route",
    )(qp, sub_keys_bf)


def _peer_kernel(idx_ref, gate_ref, h_ref, uv_ref, o_ref, buf_ref, sem_ref):
    tg, d = h_ref.shape
    eye = (lax.broadcasted_iota(jnp.int32, (PEER_SEL, PEER_SEL), 0)
           == lax.broadcasted_iota(jnp.int32, (PEER_SEL, PEER_SEL), 1))

    def issue(t, slot):
        for j in range(PEER_SEL):
            pltpu.make_async_copy(uv_ref.at[idx_ref[t, j]], buf_ref.at[slot, pl.ds(j, 1)],
                                  sem_ref.at[slot]).start(priority=j % 2)

    def wait(slot):
        pltpu.make_async_copy(uv_ref.at[pl.ds(0, PEER_SEL), 0], buf_ref.at[slot], sem_ref.at[slot]).wait()

    def activations(t, slot):
        hrow = h_ref[pl.ds(t, 1), :]
        s_col = jnp.sum(buf_ref[slot, :, :d] * hrow, axis=1, keepdims=True)
        s_row = jnp.sum(jnp.where(eye, s_col, 0.0), axis=0, keepdims=True)
        return gate_ref[pl.ds(t, 1), :] * _gelu(s_row)

    def combine(t, slot, a_row):
        a_col = jnp.sum(jnp.where(eye, a_row, 0.0), axis=1, keepdims=True)
        o_ref[pl.ds(t, 1), :] = jnp.sum(a_col * buf_ref[slot, :, d:], axis=0, keepdims=True)

    for s in range(PEER_SLOTS):
        issue(s, s)
    wait(0)
    first = activations(0, 0)

    def group(g, a_row):
        for s in range(PEER_SLOTS):
            t = g * PEER_SLOTS + s
            nxt = (s + 1) % PEER_SLOTS
            wait(nxt)
            a_next = activations(t + 1, nxt)
            combine(t, s, a_row)
            issue(t + PEER_SLOTS, s)
            a_row = a_next
        return a_row

    n_groups = tg // PEER_SLOTS
    a_row = lax.fori_loop(0, n_groups - 1, group, first)
    for s in range(PEER_SLOTS):
        t = (n_groups - 1) * PEER_SLOTS + s
        if s + 1 < PEER_SLOTS:
            wait(s + 1)
            a_next = activations(t + 1, s + 1)
        combine(t, s, a_row)
        a_row = a_next


def _peer(idx, gate, h2, uv):
    t, d = h2.shape
    tg = PEER_TOKEN_BLOCK
    assert t % tg == 0 and tg % PEER_SLOTS == 0 and uv.shape[1:] == (1, 2 * d)
    return pl.pallas_call(
        _peer_kernel,
        grid=(t // tg,),
        in_specs=[pl.BlockSpec((tg, PEER_SEL), lambda i: (i, 0), memory_space=pltpu.SMEM),
                  pl.BlockSpec((tg, PEER_SEL), lambda i: (i, 0)),
                  pl.BlockSpec((tg, d), lambda i: (i, 0)),
                  pl.BlockSpec(memory_space=pl.ANY)],
        out_specs=pl.BlockSpec((tg, d), lambda i: (i, 0)),
        out_shape=jax.ShapeDtypeStruct((t, d), F32),
        scratch_shapes=[pltpu.VMEM((PEER_SLOTS, PEER_SEL, 2 * d), F32),
                        pltpu.SemaphoreType.DMA((PEER_SLOTS,))],
        compiler_params=_params("arbitrary"),
        name="peer",
    )(idx, gate, h2, uv)


def _final_kernel(x1_ref, peer_ref, mod_ref, g_ref, y_ref):
    y_ref[...] = x1_ref[...] + mod_ref[:, 5:6, :] * _rms(peer_ref[...], g_ref[...])


def _final(x1, peer, mod, g_post2):
    n, length, d = x1.shape
    nb, tb = _token_blocking(n, length)
    tok = pl.BlockSpec((nb, tb, d), lambda b, i: (b, i, 0))
    return pl.pallas_call(
        _final_kernel,
        grid=(n // nb, length // tb),
        in_specs=[tok, tok, pl.BlockSpec((nb, 6, d), lambda b, i: (b, 0, 0)),
                  pl.BlockSpec((1, d), lambda b, i: (0, 0))],
        out_specs=tok,
        out_shape=jax.ShapeDtypeStruct((n, length, d), F32),
        compiler_params=_params("arbitrary", "arbitrary"),
        name="final",
    )(x1, peer, mod, g_post2)


def _layer(x, mod, hist_k, hist_v, hist_p, wts):
    (g_pre1, g_post1, g_pre2, g_post2, w_in_bf, g_out_a, g_out_b, w_pool_bf, pool_scale, w_out_bf,
     w_query_bf, sub_keys_bf, uv) = wts
    n, length, d = x.shape
    q_bf, k, v, p, k_bf, v_bf = _inproj(x, mod, g_pre1, w_in_bf)
    offset = 0 if hist_k is None else hist_k.shape[1]
    o_a = _attn(q_bf, k_bf, v_bf, hist_k, hist_v)
    x1, h2, qp = _post(o_a, p, hist_p, x, mod, g_out_a, g_out_b, w_pool_bf, pool_scale, w_out_bf,
                       g_post1, g_pre2, w_query_bf, offset)
    idx, gate = _route(qp.reshape(n * length, -1), sub_keys_bf)
    peer = _peer(idx, gate, h2.reshape(n * length, d), uv)
    y = _final(x1, peer.reshape(n, length, d), mod, g_post2)
    heads = (n, length, SB_HEADS, SB_HEAD_DIM)
    p_state = jnp.concatenate([hist_p[:, 1:], p], axis=1)[:, -POOL_HIST:]
    return y, k.reshape(heads), v.reshape(heads), p_state


def kernel(x_prompt, x_sample, cache_k, cache_v, state_pool, c_prompt, c_sample, w_ada, b_ada, g_pre1, g_post1, g_pre2, g_post2, w_in, g_out_a, g_out_b, w_pool, pool_scale, w_out, w_query, sub_keys, u_experts, v_experts):
    depth = w_ada.shape[0]
    assert depth == 1
    bp, bs = x_prompt.shape[0], x_sample.shape[0]
    row = lambda a: a.reshape(1, -1)
    c_all = jnp.concatenate([c_prompt, c_sample], axis=0)
    pad = (-c_all.shape[0]) % 8
    mod = _mod(jnp.pad(c_all, ((0, pad), (0, 0))), w_ada[0], b_ada[0]).reshape(-1, 6, D_MODEL)
    wts = (row(g_pre1[0]), row(g_post1[0]), row(g_pre2[0]), row(g_post2[0]), w_in[0].astype(BF16),
           row(g_out_a[0]), row(g_out_b[0]), w_pool[0].astype(BF16), row(pool_scale[0]),
           w_out[0].astype(BF16), w_query[0].astype(BF16), sub_keys[0].astype(BF16),
           jnp.concatenate([u_experts[0][:, None, :], v_experts[0][:, None, :]], axis=2))
    past = cache_k.shape[2]
    hist_p = jnp.pad(state_pool[0], ((0, 0), (POOL_TAIL - POOL_HIST, 0), (0, 0)))
    y_s, k_s, v_s, p_s = _layer(x_sample, mod[bp:bp + bs], cache_k[0].reshape(bs, past, D_ATTN),
                                cache_v[0].reshape(bs, past, D_ATTN), hist_p, wts)
    y_p, k_p, v_p, p_p = _layer(x_prompt, mod[:bp], None, None,
                                jnp.zeros((bp, POOL_TAIL, D_POOL), F32), wts)
    return (y_p, y_s, k_p[None], v_p[None], p_p[None], k_s[None], v_s[None], p_s[None])
```

```python
import functools

import jax
import jax.numpy as jnp
from jax import lax
from jax.experimental import pallas as pl
from jax.experimental.pallas import tpu as pltpu

F32 = jnp.float32
BF16 = jnp.bfloat16

D_MODEL = 1024
D_ATTN = D_MODEL // 2
SB_HEADS = 8
SB_HEAD_DIM = D_ATTN // SB_HEADS
D_POOL = D_MODEL - D_ATTN
POOL_WINDOWS = (2, 4, 8, 16)
POOL_GROUP = D_POOL // len(POOL_WINDOWS)
POOL_HIST = max(POOL_WINDOWS) - 1
POOL_TAIL = POOL_HIST + 1
D_IN = 3 * D_ATTN + D_POOL
N_KEYS = 128
PEER_HEADS = 8
PEER_TOPK = 16
KEY_HALF = 128
PEER_SEL = PEER_HEADS * PEER_TOPK
EPS = 1e-6

LANES = 128
HEADS_PER_LANE_TILE = LANES // SB_HEAD_DIM
SB_KEY_BLOCK = 128
SB_KEY_WINDOW = 256
SB_LANE_TILES = 4
EXP_UNDERFLOW = -104.0

TOKEN_BLOCK = 256
PEER_TOKEN_BLOCK = 256
PEER_SLOTS = 8
PACK_ROWS = 512
VMEM_LIMIT = 48 * 1024 * 1024


def _params(*sem):
    return pltpu.CompilerParams(dimension_semantics=sem, vmem_limit_bytes=VMEM_LIMIT)


def _rms(x, g):
    ms = jnp.mean(x * x, axis=-1, keepdims=True)
    return x * lax.rsqrt(ms + EPS) * g


def _gelu(x):
    return 0.5 * x * (1.0 + lax.erf(x * (2.0 ** -0.5)))


def _mod_kernel(c_ref, w_ref, b_ref, o_ref):
    s = jax.nn.silu(c_ref[...])
    o_ref[...] = jnp.dot(s, w_ref[...], precision=lax.Precision.HIGHEST,
                         preferred_element_type=F32) + b_ref[...]


def _mod(c, w_ada, b_ada):
    n, d = c.shape
    dout = w_ada.shape[1]
    return pl.pallas_call(
        _mod_kernel,
        grid=(dout // d,),
        in_specs=[pl.BlockSpec((n, d), lambda j: (0, 0)),
                  pl.BlockSpec((d, d), lambda j: (0, j)),
                  pl.BlockSpec((1, d), lambda j: (0, j))],
        out_specs=pl.BlockSpec((n, d), lambda j: (0, j)),
        out_shape=jax.ShapeDtypeStruct((n, dout), F32),
        compiler_params=_params("arbitrary"),
        name="mod",
    )(c, w_ada, b_ada.reshape(1, dout))


def _pack_kernel(u_ref, v_ref, out_ref, stage_ref, sem_ref, *, n_steps):
    i = pl.program_id(0)
    rows, d = u_ref.shape
    slot = lax.rem(i, 2)

    def copy(step, s):
        return pltpu.make_async_copy(stage_ref.at[s], out_ref.at[pl.ds(step * rows, rows), 0], sem_ref.at[s])

    @pl.when(i >= 2)
    def _():
        copy(i - 2, slot).wait()

    stage_ref[slot, :, :d] = u_ref[...]
    stage_ref[slot, :, d:] = v_ref[...]
    copy(i, slot).start()

    @pl.when(i == n_steps - 1)
    def _():
        if n_steps >= 2:
            copy(i - 1, 1 - slot).wait()
        copy(i, slot).wait()


def _pack_experts(u, v):
    n, d = u.shape
    rows = min(n, PACK_ROWS)
    assert n % rows == 0
    return pl.pallas_call(
        functools.partial(_pack_kernel, n_steps=n // rows),
        grid=(n // rows,),
        in_specs=[pl.BlockSpec((rows, d), lambda i: (i, 0))] * 2,
        out_specs=pl.BlockSpec(memory_space=pl.ANY),
        out_shape=jax.ShapeDtypeStruct((n, 1, 2 * d), F32),
        scratch_shapes=[pltpu.VMEM((2, rows, 2 * d), F32), pltpu.SemaphoreType.DMA((2,))],
        compiler_params=_params("arbitrary"),
        name="pack",
    )(u, v)


def _inproj_kernel(x_ref, mod_ref, g_ref, w_ref, q_ref, k_ref, v_ref, p_ref, kb_ref, vb_ref):
    nb, tb, d = x_ref.shape
    h = _rms(x_ref[...], g_ref[...]) * (1.0 + mod_ref[:, 1:2, :]) + mod_ref[:, 0:1, :]
    z = jnp.dot(h.astype(BF16).reshape(nb * tb, d), w_ref[...], preferred_element_type=F32)
    z = z.reshape(nb, tb, D_IN)
    k = z[:, :, D_ATTN:2 * D_ATTN]
    v = z[:, :, 2 * D_ATTN:3 * D_ATTN]
    q_ref[...] = (z[:, :, :D_ATTN] * (SB_HEAD_DIM ** -0.5)).astype(BF16)
    k_ref[...] = k
    v_ref[...] = v
    p_ref[...] = z[:, :, 3 * D_ATTN:]
    kb_ref[...] = k.astype(BF16)
    vb_ref[...] = v.astype(BF16)


def _token_blocking(n_batch, length):
    tb = min(length, TOKEN_BLOCK)
    nb = TOKEN_BLOCK // tb
    assert length % tb == 0 and n_batch % nb == 0 and tb % 8 == 0
    return nb, tb


def _inproj(x, mod, g_pre1, w_in_bf):
    n, length, d = x.shape
    nb, tb = _token_blocking(n, length)
    tok = lambda w: pl.BlockSpec((nb, tb, w), lambda b, i: (b, i, 0))
    out = lambda dt: jax.ShapeDtypeStruct((n, length, D_ATTN), dt)
    return pl.pallas_call(
        _inproj_kernel,
        grid=(n // nb, length // tb),
        in_specs=[tok(d),
                  pl.BlockSpec((nb, 6, d), lambda b, i: (b, 0, 0)),
                  pl.BlockSpec((1, d), lambda b, i: (0, 0)),
                  pl.BlockSpec((d, D_IN), lambda b, i: (0, 0))],
        out_specs=[tok(D_ATTN)] * 6,
        out_shape=[out(BF16), out(F32), out(F32), out(F32), out(BF16), out(BF16)],
        compiler_params=_params("arbitrary", "arbitrary"),
        name="inproj",
    )(x, mod, g_pre1, w_in_bf)


def _suffix_matrix(tk):
    shape = (2 * tk, tk + LANES)
    r = lax.broadcasted_iota(jnp.int32, shape, 0)
    r = jnp.where(r >= tk, r - tk, r)
    c = lax.broadcasted_iota(jnp.int32, shape, 1)
    return jnp.where((r > c) | (c >= tk), 1.0, 0.0).astype(BF16)


def _sb_step(z, vb, carry, suffix_mat, mask):
    tk = z.shape[1]
    sp = jnp.maximum(z, 0.0) + jnp.log1p(jnp.exp(-jnp.abs(z)))
    l = -sp if mask is None else jnp.where(mask, -sp, 0.0)
    l_hi = l.astype(BF16)
    l_lo = (l - l_hi.astype(F32)).astype(BF16)
    st = jnp.dot(jnp.concatenate([l_hi, l_lo], axis=1), suffix_mat, preferred_element_type=F32)
    e = z - sp + st[:, :tk]
    if carry is not None:
        e = e + jnp.concatenate([carry] * (tk // LANES), axis=1)
    w = jnp.exp(e)
    if mask is not None:
        w = jnp.where(mask, w, 0.0)
    pv = jnp.dot(w.astype(BF16), vb, preferred_element_type=F32)
    total = st[:, tk:]
    return (total if carry is None else carry + total), pv


def _attn_kernel(*refs, n_hist_windows):
    if n_hist_windows:
        q_ref, k_ref, v_ref, hk_ref, hv_ref, o_ref = refs
    else:
        q_ref, k_ref, v_ref, o_ref = refs
    tq = q_ref.shape[1]
    n_tiles = q_ref.shape[2] // LANES
    i = pl.program_id(2)
    nt = (((1,), (1,)), ((), ()))
    tile = lambda t: slice(t * LANES, (t + 1) * LANES)
    lane = lax.broadcasted_iota(jnp.int32, (tq, LANES), 1)
    in_head = [(lane >= h * SB_HEAD_DIM) & (lane < (h + 1) * SB_HEAD_DIM) for h in range(HEADS_PER_LANE_TILE)]
    qh = [[jnp.where(m, q_ref[0, :, tile(t)], jnp.zeros((tq, LANES), BF16)) for m in in_head]
          for t in range(n_tiles)]

    def step(read, carries, accs, mat, mask):
        new_carries, new_accs = [], []
        for t in range(n_tiles):
            kb, vb = read(t)
            pvs = []
            for h in range(HEADS_PER_LANE_TILE):
                z = lax.dot_general(qh[t][h], kb, nt, preferred_element_type=F32)
                c, pv = _sb_step(z, vb, carries[t * HEADS_PER_LANE_TILE + h], mat, mask)
                new_carries.append(c)
                pvs.append(pv)
            out = pvs[-1]
            for h in range(HEADS_PER_LANE_TILE - 1):
                out = jnp.where(in_head[h], pvs[h], out)
            new_accs.append(out if accs[t] is None else accs[t] + out)
        return tuple(new_carries), tuple(new_accs), jnp.max(functools.reduce(jnp.maximum, new_carries))

    def sweep(state, k_src, v_src, mat, partial_last):
        per = SB_KEY_WINDOW // SB_KEY_BLOCK
        col = lax.broadcasted_iota(jnp.int32, (tq, SB_KEY_WINDOW), 1)

        def body(s):
            r = s[0]
            first = jnp.maximum(r - per, 0)
            start = pl.multiple_of(first * SB_KEY_BLOCK, SB_KEY_BLOCK)
            rows = pl.ds(start, SB_KEY_WINDOW)
            mask = col < (r - first) * SB_KEY_BLOCK if partial_last else None
            read = lambda t: (k_src[0, rows, tile(t)].astype(BF16), v_src[0, rows, tile(t)].astype(BF16))
            return (r - per,) + step(read, s[1], s[2], mat, mask)
        return lax.while_loop(lambda s: (s[0] > 0) & (s[3] > EXP_UNDERFLOW), body, state)

    causal = (lax.broadcasted_iota(jnp.int32, (tq, tq), 1) < lax.broadcasted_iota(jnp.int32, (tq, tq), 0))
    rows = pl.ds(pl.multiple_of(i * tq, tq), tq)
    state = (i,) + step(lambda t: (k_ref[0, rows, tile(t)], v_ref[0, rows, tile(t)]),
                        (None,) * (n_tiles * HEADS_PER_LANE_TILE), (None,) * n_tiles, _suffix_matrix(tq), causal)
    win_mat = _suffix_matrix(SB_KEY_WINDOW)
    if tq == SB_KEY_BLOCK:
        assert k_ref.shape[1] >= SB_KEY_WINDOW
        state = sweep(state, k_ref, v_ref, win_mat, True)
    else:
        assert q_ref.shape[1] == k_ref.shape[1]
    if n_hist_windows:
        per = SB_KEY_WINDOW // SB_KEY_BLOCK
        state = sweep((jnp.int32(n_hist_windows * per),) + state[1:], hk_ref, hv_ref, win_mat, False)
    o_ref[0] = jnp.concatenate(state[2], axis=1)


def _attn(q_bf, k_bf, v_bf, hist_k=None, hist_v=None):
    n, length, _ = q_bf.shape
    tq = min(length, SB_KEY_BLOCK)
    width = SB_LANE_TILES * LANES
    qspec = pl.BlockSpec((1, tq, width), lambda b, t, i: (b, i, t))
    kspec = pl.BlockSpec((1, length, width), lambda b, t, i: (b, 0, t))
    in_specs = [qspec, kspec, kspec]
    args = [q_bf, k_bf, v_bf]
    n_hist_windows = 0
    if hist_k is not None:
        past = hist_k.shape[1]
        assert past % SB_KEY_WINDOW == 0
        n_hist_windows = past // SB_KEY_WINDOW
        hspec = pl.BlockSpec((1, past, width), lambda b, t, i: (b, 0, t))
        in_specs += [hspec, hspec]
        args += [hist_k, hist_v]
    return pl.pallas_call(
        functools.partial(_attn_kernel, n_hist_windows=n_hist_windows),
        grid=(n, D_ATTN // width, length // tq),
        in_specs=in_specs,
        out_specs=qspec,
        out_shape=jax.ShapeDtypeStruct((n, length, D_ATTN), F32),
        compiler_params=_params("arbitrary", "arbitrary", "arbitrary"),
        name="attn",
    )(*args)


def _post_kernel(oa_ref, p_ref, pprev_ref, hist_ref, x_ref, mod_ref, goa_ref, gob_ref, wpool_ref,
                 pscale_ref, wout_ref, gpost1_ref, gpre2_ref, wq_ref,
                 x1_ref, h2_ref, qp_ref, xp_ref, *, offset):
    nb, tb, d = x_ref.shape
    i = pl.program_id(1)
    p = p_ref[...]
    first = jnp.broadcast_to(i == 0, (nb, POOL_TAIL, D_POOL))
    xp_ref[:, :POOL_TAIL, :] = jnp.where(first, hist_ref[...], pprev_ref[:, tb - POOL_TAIL:, :])
    xp_ref[:, POOL_TAIL:, :] = p
    pos = offset + i * tb + lax.broadcasted_iota(jnp.int32, (1, tb, 1), 1)
    mixed = []
    for g, w in enumerate(POOL_WINDOWS):
        lanes = slice(g * POOL_GROUP, (g + 1) * POOL_GROUP)
        s = p[:, :, lanes]
        for back in range(1, w):
            s = s + xp_ref[:, pl.ds(POOL_TAIL - back, tb), lanes]
        cnt = jnp.minimum(pos + 1, w).astype(F32)
        pooled = (s / cnt - p[:, :, lanes]).reshape(nb * tb, POOL_GROUP)
        mixed.append(jnp.dot(pooled.astype(BF16), wpool_ref[g], preferred_element_type=F32))
    o_b = jnp.concatenate(mixed, axis=-1) * pscale_ref[...]
    o_a = oa_ref[...].reshape(nb * tb, D_ATTN)
    cat = jnp.concatenate([_rms(o_a, goa_ref[...]), _rms(o_b, gob_ref[...])], axis=-1)
    o = jnp.dot(cat.astype(BF16), wout_ref[...], preferred_element_type=F32)
    x1 = x_ref[...] + mod_ref[:, 2:3, :] * _rms(o, gpost1_ref[...]).reshape(nb, tb, d)
    x1_ref[...] = x1
    h2 = _rms(x1, gpre2_ref[...]) * (1.0 + mod_ref[:, 4:5, :]) + mod_ref[:, 3:4, :]
    h2_ref[...] = h2
    qp = jnp.dot(h2.astype(BF16).reshape(nb * tb, d), wq_ref[...], preferred_element_type=F32)
    qp_ref[...] = qp.reshape(nb, tb, qp_ref.shape[2])


def _post(o_a, p, hist, x, mod, g_out_a, g_out_b, w_pool_bf, pool_scale, w_out_bf, g_post1, g_pre2,
          w_query_bf, offset):
    n, length, d = x.shape
    nb, tb = _token_blocking(n, length)
    assert tb >= POOL_TAIL
    dq = w_query_bf.shape[1]
    tok = lambda w: pl.BlockSpec((nb, tb, w), lambda b, i: (b, i, 0))
    full = lambda a: pl.BlockSpec(a.shape, lambda b, i: (0,) * a.ndim)
    return pl.pallas_call(
        functools.partial(_post_kernel, offset=offset),
        grid=(n // nb, length // tb),
        in_specs=[tok(D_ATTN), tok(D_POOL),
                  pl.BlockSpec((nb, tb, D_POOL), lambda b, i: (b, jnp.maximum(i - 1, 0), 0)),
                  pl.BlockSpec((nb, POOL_TAIL, D_POOL), lambda b, i: (b, 0, 0)),
                  tok(d),
                  pl.BlockSpec((nb, 6, d), lambda b, i: (b, 0, 0)),
                  full(g_out_a), full(g_out_b), full(w_pool_bf), full(pool_scale), full(w_out_bf),
                  full(g_post1), full(g_pre2), full(w_query_bf)],
        out_specs=[tok(d), tok(d), tok(dq)],
        out_shape=[jax.ShapeDtypeStruct((n, length, d), F32),
                   jax.ShapeDtypeStruct((n, length, d), F32),
                   jax.ShapeDtypeStruct((n, length, dq), F32)],
        scratch_shapes=[pltpu.VMEM((nb, tb + POOL_TAIL, D_POOL), F32)],
        compiler_params=_params("arbitrary", "arbitrary"),
        name="post",
    )(o_a, p, p, hist, x, mod, g_out_a, g_out_b, w_pool_bf, pool_scale, w_out_bf, g_post1, g_pre2,
      w_query_bf)


def _topk_rows(s, k, order):
    vals, picks = [], []
    for _ in range(k):
        m = jnp.max(s, axis=0, keepdims=True)
        pick = jnp.min(jnp.where(s == m, order, jnp.inf), axis=0, keepdims=True)
        vals.append(m)
        picks.append(pick)
        s = jnp.where(order == pick, -jnp.inf, s)
    return jnp.concatenate(vals, axis=0), jnp.concatenate(picks, axis=0).astype(jnp.int32)


def _select_rows(table, sel):
    out = jnp.zeros_like(table)
    for r in range(table.shape[0]):
        out = jnp.where(sel == r, table[r:r + 1, :], out)
    return out


def _pair_candidates(v1, v2):
    tb = v1.shape[1]
    sub = lambda n: lax.broadcasted_iota(jnp.int32, (n, tb), 0)
    vals, flat = [], []
    for a, nb in ((0, 16), (1, 8), (2, 8), (3, 8)):
        vals.append(v1[a:a + 1, :] + v2[:nb, :])
        flat.append(a * PEER_TOPK + sub(nb))
    vals.append(v1[8:16, :] + v2[0:1, :])
    flat.append((8 + sub(8)) * PEER_TOPK)
    for b in range(3):
        vals.append(jnp.where(sub(8) >= 4, v1[0:8, :] + v2[b:b + 1, :], -jnp.inf))
        flat.append(sub(8) * PEER_TOPK + b)
    return jnp.concatenate(vals, axis=0), jnp.concatenate(flat, axis=0).astype(F32)


def _route_kernel(qp_ref, keys_ref, idx_ref, gate_ref):
    nt = (((1,), (1,)), ((), ()))
    tb = qp_ref.shape[0]
    key_order = lax.broadcasted_iota(jnp.int32, (N_KEYS, tb), 0).astype(F32)
    idx_rows, gate_rows = [], []
    for h in range(PEER_HEADS):
        base = h * 2 * KEY_HALF
        q1 = qp_ref[:, base:base + KEY_HALF].astype(BF16)
        q2 = qp_ref[:, base + KEY_HALF:base + 2 * KEY_HALF].astype(BF16)
        s1 = lax.dot_general(keys_ref[0, h], q1, nt, preferred_element_type=F32)
        s2 = lax.dot_general(keys_ref[1, h], q2, nt, preferred_element_type=F32)
        v1, i1 = _topk_rows(s1, PEER_TOPK, key_order)
        v2, i2 = _topk_rows(s2, PEER_TOPK, key_order)
        cand, flat = _pair_candidates(v1, v2)
        best, sel = _topk_rows(cand, PEER_TOPK, flat)
        a_sel = lax.shift_right_logical(sel, 4)
        b_sel = sel & (PEER_TOPK - 1)
        idx_rows.append(_select_rows(i1, a_sel) * N_KEYS + _select_rows(i2, b_sel))
        e = jnp.exp(best - best[0:1, :])
        gate_rows.append(e / jnp.sum(e, axis=0, keepdims=True))
    idx_ref[...] = jnp.concatenate(idx_rows, axis=0).T
    gate_ref[...] = jnp.concatenate(gate_rows, axis=0).T


def _route(qp, sub_keys_bf):
    t, dq = qp.shape
    tb = min(t, TOKEN_BLOCK)
    assert t % tb == 0 and PEER_TOPK == 16
    return pl.pallas_call(
        _route_kernel,
        grid=(t // tb,),
        in_specs=[pl.BlockSpec((tb, dq), lambda i: (i, 0)),
                  pl.BlockSpec(sub_keys_bf.shape, lambda i: (0, 0, 0, 0))],
        out_specs=[pl.BlockSpec((tb, PEER_SEL), lambda i: (i, 0))] * 2,
        out_shape=[jax.ShapeDtypeStruct((t, PEER_SEL), jnp.int32),
                   jax.ShapeDtypeStruct((t, PEER_SEL), F32)],
        compiler_params=_params("arbitrary"),
        name="route",
    )(qp, sub_keys_bf)


def _peer_kernel(idx_ref, gate_ref, h_ref, uv_ref, o_ref, buf_ref, sem_ref):
    tg, d = h_ref.shape
    eye = (lax.broadcasted_iota(jnp.int32, (PEER_SEL, PEER_SEL), 0)
           == lax.broadcasted_iota(jnp.int32, (PEER_SEL, PEER_SEL), 1))

    def issue(t, slot):
        for j in range(PEER_SEL):
            pltpu.make_async_copy(uv_ref.at[idx_ref[t, j]], buf_ref.at[slot, pl.ds(j, 1)],
                                  sem_ref.at[slot]).start(priority=j % 2)

    def wait(slot):
        pltpu.make_async_copy(uv_ref.at[pl.ds(0, PEER_SEL), 0], buf_ref.at[slot], sem_ref.at[slot]).wait()

    def activations(t, slot):
        hrow = h_ref[pl.ds(t, 1), :]
        s_col = jnp.sum(buf_ref[slot, :, :d] * hrow, axis=1, keepdims=True)
        s_row = jnp.sum(jnp.where(eye, s_col, 0.0), axis=0, keepdims=True)
        return gate_ref[pl.ds(t, 1), :] * _gelu(s_row)

    def combine(t, slot, a_row):
        a_col = jnp.sum(jnp.where(eye, a_row, 0.0), axis=1, keepdims=True)
        o_ref[pl.ds(t, 1), :] = jnp.sum(a_col * buf_ref[slot, :, d:], axis=0, keepdims=True)

    for s in range(PEER_SLOTS):
        issue(s, s)
    wait(0)
    first = activations(0, 0)

    def group(g, a_row):
        for s in range(PEER_SLOTS):
            t = g * PEER_SLOTS + s
            nxt = (s + 1) % PEER_SLOTS
            wait(nxt)
            a_next = activations(t + 1, nxt)
            combine(t, s, a_row)
            issue(t + PEER_SLOTS, s)
            a_row = a_next
        return a_row

    n_groups = tg // PEER_SLOTS
    a_row = lax.fori_loop(0, n_groups - 1, group, first)
    for s in range(PEER_SLOTS):
        t = (n_groups - 1) * PEER_SLOTS + s
        if s + 1 < PEER_SLOTS:
            wait(s + 1)
            a_next = activations(t + 1, s + 1)
        combine(t, s, a_row)
        a_row = a_next


def _peer(idx, gate, h2, uv):
    t, d = h2.shape
    tg = PEER_TOKEN_BLOCK
    assert t % tg == 0 and tg % PEER_SLOTS == 0 and uv.shape[1:] == (1, 2 * d)
    return pl.pallas_call(
        _peer_kernel,
        grid=(t // tg,),
        in_specs=[pl.BlockSpec((tg, PEER_SEL), lambda i: (i, 0), memory_space=pltpu.SMEM),
                  pl.BlockSpec((tg, PEER_SEL), lambda i: (i, 0)),
                  pl.BlockSpec((tg, d), lambda i: (i, 0)),
                  pl.BlockSpec(memory_space=pl.ANY)],
        out_specs=pl.BlockSpec((tg, d), lambda i: (i, 0)),
        out_shape=jax.ShapeDtypeStruct((t, d), F32),
        scratch_shapes=[pltpu.VMEM((PEER_SLOTS, PEER_SEL, 2 * d), F32),
                        pltpu.SemaphoreType.DMA((PEER_SLOTS,))],
        compiler_params=_params("arbitrary"),
        name="peer",
    )(idx, gate, h2, uv)


def _final_kernel(x1_ref, peer_ref, mod_ref, g_ref, y_ref):
    y_ref[...] = x1_ref[...] + mod_ref[:, 5:6, :] * _rms(peer_ref[...], g_ref[...])


def _final(x1, peer, mod, g_post2):
    n, length, d = x1.shape
    nb, tb = _token_blocking(n, length)
    tok = pl.BlockSpec((nb, tb, d), lambda b, i: (b, i, 0))
    return pl.pallas_call(
        _final_kernel,
        grid=(n // nb, length // tb),
        in_specs=[tok, tok, pl.BlockSpec((nb, 6, d), lambda b, i: (b, 0, 0)),
                  pl.BlockSpec((1, d), lambda b, i: (0, 0))],
        out_specs=tok,
        out_shape=jax.ShapeDtypeStruct((n, length, d), F32),
        compiler_params=_params("arbitrary", "arbitrary"),
        name="final",
    )(x1, peer, mod, g_post2)


def _layer(x, mod, hist_k, hist_v, hist_p, wts):
    (g_pre1, g_post1, g_pre2, g_post2, w_in_bf, g_out_a, g_out_b, w_pool_bf, pool_scale, w_out_bf,
     w_query_bf, sub_keys_bf, uv) = wts
    n, length, d = x.shape
    q_bf, k, v, p, k_bf, v_bf = _inproj(x, mod, g_pre1, w_in_bf)
    offset = 0 if hist_k is None else hist_k.shape[1]
    o_a = _attn(q_bf, k_bf, v_bf, hist_k, hist_v)
    x1, h2, qp = _post(o_a, p, hist_p, x, mod, g_out_a, g_out_b, w_pool_bf, pool_scale, w_out_bf,
                       g_post1, g_pre2, w_query_bf, offset)
    idx, gate = _route(qp.reshape(n * length, -1), sub_keys_bf)
    peer = _peer(idx, gate, h2.reshape(n * length, d), uv)
    y = _final(x1, peer.reshape(n, length, d), mod, g_post2)
    heads = (n, length, SB_HEADS, SB_HEAD_DIM)
    p_state = jnp.concatenate([hist_p[:, 1:], p], axis=1)[:, -POOL_HIST:]
    return y, k.reshape(heads), v.reshape(heads), p_state


def kernel(x_prompt, x_sample, cache_k, cache_v, state_pool, c_prompt, c_sample, w_ada, b_ada, g_pre1, g_post1, g_pre2, g_post2, w_in, g_out_a, g_out_b, w_pool, pool_scale, w_out, w_query, sub_keys, u_experts, v_experts):
    depth = w_ada.shape[0]
    assert depth == 1
    bp, bs = x_prompt.shape[0], x_sample.shape[0]
    row = lambda a: a.reshape(1, -1)
    c_all = jnp.concatenate([c_prompt, c_sample], axis=0)
    pad = (-c_all.shape[0]) % 8
    mod = _mod(jnp.pad(c_all, ((0, pad), (0, 0))), w_ada[0], b_ada[0]).reshape(-1, 6, D_MODEL)
    wts = (row(g_pre1[0]), row(g_post1[0]), row(g_pre2[0]), row(g_post2[0]), w_in[0].astype(BF16),
           row(g_out_a[0]), row(g_out_b[0]), w_pool[0].astype(BF16), row(pool_scale[0]),
           w_out[0].astype(BF16), w_query[0].astype(BF16), sub_keys[0].astype(BF16),
           _pack_experts(u_experts[0], v_experts[0]))
    past = cache_k.shape[2]
    hist_p = jnp.pad(state_pool[0], ((0, 0), (POOL_TAIL - POOL_HIST, 0), (0, 0)))
    y_s, k_s, v_s, p_s = _layer(x_sample, mod[bp:bp + bs], cache_k[0].reshape(bs, past, D_ATTN),
                                cache_v[0].reshape(bs, past, D_ATTN), hist_p, wts)
    y_p, k_p, v_p, p_p = _layer(x_prompt, mod[:bp], None, None,
                                jnp.zeros((bp, POOL_TAIL, D_POOL), F32), wts)
    return (y_p, y_s, k_p[None], v_p[None], p_p[None], k_s[None], v_s[None], p_s[None])
```
